```python
import math
import jax
import jax.numpy as jnp
from jax import lax
import numpy as np

D_MODEL = 4096
BATCH = 1
SEQ = 16384
DEPTH = 2
DEC_BATCH = 1
DEC_SEQ = 8192
PAST_LEN = 128

D_MIX = D_MODEL
N_MIX_GROUPS = 4
GROUP_W = D_MIX // N_MIX_GROUPS
HEAD_DIM = 128
N_RET_HEADS = GROUP_W // HEAD_DIM
N_MLSTM_HEADS = GROUP_W // HEAD_DIM
N_NA_HEADS = GROUP_W // HEAD_DIM
S5_CH = 16
S5_GROUPS = GROUP_W // S5_CH
S5_STATE = 64
CONV_W = 3
CHUNK = 128
GRID_W = 64
NA_KH = 8
NA_KW = 16
N_EXPERTS = 16
EC_CAPACITY = 2
D_EXPERT = D_MODEL // 2
PLE_DIM = 256
D_IN = 12 * GROUP_W + 4 * N_MLSTM_HEADS
ROPE_BASE = 10000.0
EPS = 1e-6
M_INIT = -1e30
LAMBDA_RE_MAX = -1e-4

kernel_name = "hybrid_bidir_encoder_two_groups"


def rmsnorm(x, g):
    xf = x.astype(jnp.float32)
    y = xf * lax.rsqrt(jnp.mean(xf * xf, axis=-1, keepdims=True) + EPS)
    return (y * g.astype(jnp.float32)).astype(x.dtype)


def to_heads(x, n_heads):
    B, T, _ = x.shape
    return x.reshape(B, T, n_heads, HEAD_DIM).transpose(0, 2, 1, 3)


def rotary(x):
    T, dh = x.shape[1], x.shape[-1]
    half = dh // 2
    inv_freq = ROPE_BASE ** (-jnp.arange(half, dtype=jnp.float32) / half)
    ang = jnp.arange(T, dtype=jnp.float32)[:, None] * inv_freq[None, :]
    cos = jnp.cos(ang)[None, :, None, :]
    sin = jnp.sin(ang)[None, :, None, :]
    xf = x.astype(jnp.float32)
    x1, x2 = xf[..., :half], xf[..., half:]
    return jnp.concatenate([x1 * cos - x2 * sin, x2 * cos + x1 * sin], axis=-1)


def centred_dwconv(x, w, b):
    T = x.shape[1]
    pad = w.shape[0] // 2
    xp = jnp.pad(x, ((0, 0), (pad, pad), (0, 0)))
    out = b
    for j in range(w.shape[0]):
        out = out + xp[:, j:j + T] * w[j]
    return out


def chunked_gated_memory(q, k, v, log_f, log_i, with_norm):
    f32 = jnp.float32
    B, H, T, dh = q.shape
    nc = T // CHUNK
    qc = q.astype(f32).reshape(B, H, nc, CHUNK, dh)
    kc = k.astype(f32).reshape(B, H, nc, CHUNK, dh)
    vc = v.astype(f32).reshape(B, H, nc, CHUNK, dh)
    lf = log_f.astype(f32).reshape(B, H, nc, CHUNK)
    li = log_i.astype(f32).reshape(B, H, nc, CHUNK)
    b = jnp.cumsum(lf, axis=-1)
    g = b[..., -1]
    a = g[..., None] - b + li

    def step(carry, xs):
        C, n, m = carry
        k_j, v_j, a_j, g_j = xs
        m_new = jnp.maximum(g_j + m, jnp.max(a_j, axis=-1))
        decay = jnp.exp(g_j + m - m_new)
        w = jnp.exp(a_j - m_new[..., None])
        C_new = decay[..., None, None] * C + jnp.einsum('bhl,bhlk,bhlv->bhkv', w, k_j, v_j)
        n_new = decay[..., None] * n + jnp.einsum('bhl,bhlk->bhk', w, k_j)
        return (C_new, n_new, m_new), (C, n, m)

    init = (jnp.zeros((B, H, dh, dh), f32), jnp.zeros((B, H, dh), f32), jnp.full((B, H), M_INIT, f32))
    xs = (jnp.moveaxis(kc, 2, 0), jnp.moveaxis(vc, 2, 0), jnp.moveaxis(a, 2, 0), jnp.moveaxis(g, 2, 0))
    _, (C_prev, n_prev, m_prev) = lax.scan(step, init, xs)
    C_prev = jnp.moveaxis(C_prev, 0, 2)
    n_prev = jnp.moveaxis(n_prev, 0, 2)
    m_prev = jnp.moveaxis(m_prev, 0, 2)

    lower = jnp.tril(jnp.ones((CHUNK, CHUNK), dtype=bool))
    d_log = jnp.where(lower, b[..., :, None] - b[..., None, :] + li[..., None, :], -jnp.inf)
    inter_log = b + m_prev[..., None]
    m = jnp.maximum(inter_log, jnp.max(d_log, axis=-1))
    w_intra = jnp.exp(d_log - m[..., None])
    w_inter = jnp.exp(inter_log - m)
    s = jnp.einsum('bhcid,bhcjd->bhcij', qc, kc) * w_intra
    num = (jnp.einsum('bhcij,bhcjd->bhcid', s, vc)
           + w_inter[..., None] * jnp.einsum('bhcik,bhckv->bhciv', qc, C_prev))
    num = num.reshape(B, H, T, dh)
    m = m.reshape(B, H, T)
    if with_norm:
        qn = s.sum(-1) + w_inter * jnp.einsum('bhcik,bhck->bhci', qc, n_prev)
        return num, qn.reshape(B, H, T), m
    return num, None, m


def flipper(flip):
    return (lambda t: jnp.flip(t, axis=2)) if flip else (lambda t: t)


def retention(q, k, v, g, decay_logit):
    B, T, _ = q.shape
    H = N_RET_HEADS
    qh = rotary(q.reshape(B, T, H, HEAD_DIM)).transpose(0, 2, 1, 3)
    kh = (rotary(k.reshape(B, T, H, HEAD_DIM)) * HEAD_DIM ** -0.5).transpose(0, 2, 1, 3)
    vh = to_heads(v, H)
    log_gamma = jax.nn.log_sigmoid(decay_logit.astype(jnp.float32))
    log_i = jnp.zeros((B, H, T), jnp.float32)

    def direction(lg, flip):
        fl = flipper(flip)
        log_f = jnp.broadcast_to(lg[None, :, None], (B, H, T))
        num, _, m = chunked_gated_memory(fl(qh), fl(kh), fl(vh), log_f, log_i, False)
        return fl(num * jnp.exp(m)[..., None])

    o = direction(log_gamma[0], False) + direction(log_gamma[1], True)
    mu = jnp.mean(o, axis=-1, keepdims=True)
    var = jnp.mean(jnp.square(o - mu), axis=-1, keepdims=True)
    o = (o - mu) * lax.rsqrt(var + EPS)
    o = o.transpose(0, 2, 1, 3).reshape(B, T, GROUP_W)
    return (jax.nn.silu(g.astype(jnp.float32)) * o).astype(q.dtype)


def mlstm(q, k, v, o, gates, gate_bias, conv_w, conv_b):
    f32 = jnp.float32
    B, T, _ = q.shape
    H = N_MLSTM_HEADS
    qk = jax.nn.silu(centred_dwconv(jnp.concatenate([q, k], axis=-1), conv_w, conv_b))
    qh = to_heads(qk[..., :GROUP_W], H)
    kh = to_heads(qk[..., GROUP_W:], H) * HEAD_DIM ** -0.5
    vh = to_heads(v, H)
    pre = gates.astype(f32).reshape(B, T, 4, H) + gate_bias.astype(f32)
    pre = jnp.transpose(pre, (2, 0, 3, 1))

    def direction(i_pre, f_pre, flip):
        fl = flipper(flip)
        num, qn, m = chunked_gated_memory(fl(qh), fl(kh), fl(vh), fl(jax.nn.log_sigmoid(f_pre)), fl(i_pre), True)
        return fl(num / jnp.maximum(jnp.abs(qn), jnp.exp(-m))[..., None])

    h = direction(pre[0], pre[1], False) + direction(pre[2], pre[3], True)
    h = h.transpose(0, 2, 1, 3).reshape(B, T, GROUP_W)
    return (jax.nn.sigmoid(o.astype(f32)) * h).astype(q.dtype)


def s5_direction(u, lam_re, lam_im, log_step, b_re, b_im, c_re, c_im):
    step = jnp.exp(log_step)[:, None]
    lre = jnp.minimum(lam_re, LAMBDA_RE_MAX)
    mag = jnp.exp(lre * step)
    ang = lam_im * step
    abar_re, abar_im = mag * jnp.cos(ang), mag * jnp.sin(ang)
    den = lre * lre + lam_im * lam_im
    zr, zi = abar_re - 1.0, abar_im
    coef_re = (zr * lre + zi * lam_im) / den
    coef_im = (zi * lre - zr * lam_im) / den
    bbar_re = coef_re[..., None] * b_re - coef_im[..., None] * b_im
    bbar_im = coef_re[..., None] * b_im + coef_im[..., None] * b_re
    bu_re = jnp.einsum('btgc,gpc->btgp', u, bbar_re)
    bu_im = jnp.einsum('btgc,gpc->btgp', u, bbar_im)
    a_re = jnp.broadcast_to(abar_re, bu_re.shape)
    a_im = jnp.broadcast_to(abar_im, bu_im.shape)

    def combine(e1, e2):
        a1r, a1i, b1r, b1i = e1
        a2r, a2i, b2r, b2i = e2
        return (a2r * a1r - a2i * a1i, a2r * a1i + a2i * a1r,
                a2r * b1r - a2i * b1i + b2r, a2r * b1i + a2i * b1r + b2i)

    _, _, s_re, s_im = lax.associative_scan(combine, (a_re, a_im, bu_re, bu_im), axis=1)
    return jnp.einsum('gcp,btgp->btgc', c_re, s_re) - jnp.einsum('gcp,btgp->btgc', c_im, s_im)


def s5_mixer(u, lam_re, lam_im, log_step, b_re, b_im, c_re, c_im, d_skip, glu_w, glu_b):
    f32 = jnp.float32
    B, T, _ = u.shape
    uf = u.astype(f32).reshape(B, T, S5_GROUPS, S5_CH)
    p = [t.astype(f32) for t in (lam_re, lam_im, log_step, b_re, b_im, c_re, c_im)]
    y_f = s5_direction(uf, *[t[0] for t in p])
    y_b = jnp.flip(s5_direction(jnp.flip(uf, axis=1), *[t[1] for t in p]), axis=1)
    y = (y_f + y_b).reshape(B, T, GROUP_W) + d_skip.astype(f32) * u.astype(f32)
    y = jax.nn.gelu(y)
    out = y * jax.nn.sigmoid(y @ glu_w.astype(f32) + glu_b.astype(f32))
    return out.astype(u.dtype)


def neighbourhood_attention(q, k, v, rel_bias):
    B, T, _ = q.shape
    H = N_NA_HEADS
    rows = T // GRID_W
    kh, kw = min(NA_KH, rows), NA_KW
    qg = (q * HEAD_DIM ** -0.5).reshape(B, rows, GRID_W, H, HEAD_DIM)
    kg = k.reshape(B, rows, GRID_W, H, HEAD_DIM)
    vg = v.reshape(B, rows, GRID_W, H, HEAD_DIM)
    cols = jnp.arange(GRID_W)
    col_start = jnp.clip(cols - kw // 2, 0, GRID_W - kw)
    col_idx = col_start[:, None] + jnp.arange(kw)[None, :]
    dc = col_idx - cols[:, None] + (NA_KW - 1)

    def one_row(r):
        rs = jnp.clip(r - kh // 2, 0, rows - kh)
        k_win = lax.dynamic_slice_in_dim(kg, rs, kh, axis=1)[:, :, col_idx]
        v_win = lax.dynamic_slice_in_dim(vg, rs, kh, axis=1)[:, :, col_idx]
        q_row = lax.dynamic_index_in_dim(qg, r, axis=1, keepdims=False)
        dr = rs + jnp.arange(kh) - r + (NA_KH - 1)
        bias = rel_bias[:, dr[None, :, None], dc[:, None, :]]
        s = jnp.einsum('bchd,bicjhd->bhcij', q_row, k_win).astype(jnp.float32) + bias.astype(jnp.float32)[None]
        p = jax.nn.softmax(s.reshape(B, H, GRID_W, kh * kw), axis=-1).reshape(B, H, GRID_W, kh, kw)
        return jnp.einsum('bhcij,bicjhd->bchd', p.astype(v.dtype), v_win)

    outs = lax.map(one_row, jnp.arange(rows))
    return jnp.moveaxis(outs, 0, 1).reshape(B, T, GROUP_W).astype(q.dtype)


def expert_choice_ffn(x, w_router, w_gate, w_up, w_down):
    B, T, D = x.shape
    n_tok = B * T
    cap = EC_CAPACITY * n_tok // N_EXPERTS
    xt = x.reshape(n_tok, D)
    aff = jax.nn.softmax((xt @ w_router).astype(jnp.float32), axis=-1)
    gate, idx = lax.top_k(aff.T, cap)
    xe = xt[idx]
    hid = jax.nn.silu(jnp.einsum('ecd,edf->ecf', xe, w_gate)) * jnp.einsum('ecd,edf->ecf', xe, w_up)
    ye = (jnp.einsum('ecf,efd->ecd', hid, w_down) * gate[..., None]).astype(xt.dtype)
    out = jnp.zeros_like(xt).at[idx.reshape(-1)].add(ye.reshape(-1, D))
    return out.reshape(B, T, D)


def split_projection(proj):
    sizes = (GROUP_W,) * 8 + (4 * N_MLSTM_HEADS,) + (GROUP_W,) * 4
    return jnp.split(proj, np.cumsum(sizes)[:-1].tolist(), axis=-1)


def setup_inputs(seed: int = 0) -> dict:
    key = jax.random.key(seed)
    ks = jax.random.split(key, 32)
    f32 = jnp.float32

    def nrm(k, shape, scale):
        return scale * jax.random.normal(k, shape, f32)

    def gain(k, shape):
        return 1.0 + 0.02 * jax.random.normal(k, shape, f32)

    G, P, Hm = S5_GROUPS, S5_STATE, N_MLSTM_HEADS
    ret_base = jnp.log(2.0 ** (5.0 + jnp.arange(N_RET_HEADS, dtype=f32)) - 1.0)
    gate_noise = nrm(ks[9], (DEPTH, 2, 2, Hm), 0.1)
    i_bias = gate_noise[:, :, 0]
    f_bias = jnp.linspace(3.0, 6.0, Hm, dtype=f32) + gate_noise[:, :, 1]
    mlstm_gate_bias = jnp.stack([i_bias[:, 0], f_bias[:, 0], i_bias[:, 1], f_bias[:, 1]], axis=1)
    return {
        "x_prompt": nrm(ks[0], (BATCH, SEQ, D_MODEL), 1.0),
        "x_sample": nrm(ks[1], (DEC_BATCH, DEC_SEQ, D_MODEL), 1.0),
        "p_prompt": nrm(ks[2], (DEPTH, BATCH, SEQ, PLE_DIM), 1.0),
        "p_sample": nrm(ks[3], (DEPTH, DEC_BATCH, DEC_SEQ, PLE_DIM), 1.0),
        "norm_mix": gain(ks[4], (DEPTH, D_MODEL)),
        "w_in": nrm(ks[5], (DEPTH, D_MODEL, D_IN), D_MODEL ** -0.5),
        "ret_decay_logit": ret_base[None, None, :] + nrm(ks[6], (DEPTH, 2, N_RET_HEADS), 0.05),
        "mlstm_conv_w": nrm(ks[7], (DEPTH, CONV_W, 2 * GROUP_W), CONV_W ** -0.5),
        "mlstm_conv_b": nrm(ks[8], (DEPTH, 2 * GROUP_W), 0.01),
        "mlstm_gate_bias": mlstm_gate_bias,
        "s5_lambda_re": -0.5 + nrm(ks[10], (DEPTH, 2, G, P), 0.01),
        "s5_lambda_im": jnp.pi * jnp.arange(P, dtype=f32) + nrm(ks[11], (DEPTH, 2, G, P), 0.01),
        "s5_log_step": jax.random.uniform(ks[12], (DEPTH, 2, G), f32, math.log(1e-3), math.log(1e-1)),
        "s5_b_re": nrm(ks[13], (DEPTH, 2, G, P, S5_CH), (2 * S5_CH) ** -0.5),
        "s5_b_im": nrm(ks[14], (DEPTH, 2, G, P, S5_CH), (2 * S5_CH) ** -0.5),
        "s5_c_re": nrm(ks[15], (DEPTH, 2, G, S5_CH, P), (2 * P) ** -0.5),
        "s5_c_im": nrm(ks[16], (DEPTH, 2, G, S5_CH, P), (2 * P) ** -0.5),
        "s5_d": nrm(ks[17], (DEPTH, GROUP_W), 1.0),
        "s5_glu_w": nrm(ks[18], (DEPTH, GROUP_W, GROUP_W), GROUP_W ** -0.5),
        "s5_glu_b": nrm(ks[19], (DEPTH, GROUP_W), 0.01),
        "na_rel_bias": nrm(ks[20], (DEPTH, N_NA_HEADS, 2 * NA_KH - 1, 2 * NA_KW - 1), 0.02),
        "mix_out_norm": gain(ks[21], (DEPTH, D_MIX)),
        "w_out": nrm(ks[22], (DEPTH, D_MIX, D_MODEL), D_MIX ** -0.5),
        "norm_ffn": gain(ks[23], (DEPTH, D_MODEL)),
        "w_router": nrm(ks[24], (DEPTH, D_MODEL, N_EXPERTS), D_MODEL ** -0.5),
        "w_expert_gate": nrm(ks[25], (DEPTH, N_EXPERTS, D_MODEL, D_EXPERT), D_MODEL ** -0.5),
        "w_expert_up": nrm(ks[26], (DEPTH, N_EXPERTS, D_MODEL, D_EXPERT), D_MODEL ** -0.5),
        "w_expert_down": nrm(ks[27], (DEPTH, N_EXPERTS, D_EXPERT, D_MODEL), D_EXPERT ** -0.5),
        "ple_norm": gain(ks[28], (DEPTH, D_MODEL)),
        "w_ple": nrm(ks[29], (DEPTH, PLE_DIM, D_MODEL), PLE_DIM ** -0.5),
        "w_ple_gate": nrm(ks[30], (DEPTH, D_MODEL, D_MODEL), D_MODEL ** -0.5),
        "norm_final": gain(ks[31], (D_MODEL,)),
    }


def reference(x_prompt, x_sample, p_prompt, p_sample, norm_mix, w_in, ret_decay_logit, mlstm_conv_w,
              mlstm_conv_b, mlstm_gate_bias, s5_lambda_re, s5_lambda_im, s5_log_step, s5_b_re, s5_b_im,
              s5_c_re, s5_c_im, s5_d, s5_glu_w, s5_glu_b, na_rel_bias, mix_out_norm, w_out, norm_ffn,
              w_router, w_expert_gate, w_expert_up, w_expert_down, ple_norm, w_ple, w_ple_gate, norm_final):

    def run(h, p):
        B, T, _ = h.shape
        for l in range(DEPTH):
            xn = rmsnorm(h, norm_mix[l])
            (rq, rk, rv, rg, mq, mk, mv, mo, mgates, su, nq, nk, nv) = split_projection(xn @ w_in[l])
            o_ret = retention(rq, rk, rv, rg, ret_decay_logit[l])
            o_ml = mlstm(mq, mk, mv, mo, mgates, mlstm_gate_bias[l], mlstm_conv_w[l], mlstm_conv_b[l])
            o_s5 = s5_mixer(su, s5_lambda_re[l], s5_lambda_im[l], s5_log_step[l], s5_b_re[l], s5_b_im[l],
                            s5_c_re[l], s5_c_im[l], s5_d[l], s5_glu_w[l], s5_glu_b[l])
            o_na = neighbourhood_attention(nq, nk, nv, na_rel_bias[l])
            groups = jnp.stack([o_ret, o_ml, o_s5, o_na], axis=2)
            merged = rmsnorm(groups, mix_out_norm[l].reshape(N_MIX_GROUPS, GROUP_W)).reshape(B, T, D_MIX)
            h = h + merged @ w_out[l]
            h = h + expert_choice_ffn(rmsnorm(h, norm_ffn[l]), w_router[l], w_expert_gate[l],
                                      w_expert_up[l], w_expert_down[l])
            h = h + (p[l] @ w_ple[l]) * jax.nn.sigmoid(rmsnorm(h, ple_norm[l]) @ w_ple_gate[l])
        return rmsnorm(h, norm_final)

    y_prompt = run(x_prompt, p_prompt)
    y_sample = run(x_sample, p_sample)
    return (y_prompt, y_sample)
```

```python
import functools
import math

import numpy as np
import jax
import jax.numpy as jnp
from jax import lax
from jax.experimental import pallas as pl
from jax.experimental.pallas import tpu as pltpu

F32 = jnp.float32
BF16 = jnp.bfloat16

D_MODEL = 4096
GROUP_W = 1024
HEAD_DIM = 128
N_HEADS = GROUP_W // HEAD_DIM
S5_CH = 16
S5_GROUPS = GROUP_W // S5_CH
S5_STATE = 64
GRID_W = 64
NA_KH = 8
NA_KW = 16
N_EXPERTS = 16
EC_CAPACITY = 2
D_EXPERT = D_MODEL // 2
PLE_DIM = 256
ROPE_BASE = 10000.0
EPS = 1e-6
M_INIT = -1e30
NEG = -1e30
LAMBDA_RE_MAX = -1e-4
N_MAIN = 12 * GROUP_W
N_GATE = 4 * N_HEADS
LANES = 128
MIB = 1024 * 1024

CB_RQ, CB_RK, CB_RV, CB_RG = 0, 8, 16, 24
CB_MQ, CB_MK, CB_MV, CB_MO = 32, 40, 48, 56
CB_SU = 64
CB_NQ, CB_NK, CB_NV = 72, 80, 88

S5_LC = 64
RET_L = 256
ML_L = 128
NA_QR = 8
NA_KR = 4
MOE_TM = 512
COMB_TB = 512
COMB_W = 128
ROW_CHUNK = 128


def _cparams(sem, vmem_mib):
    return pltpu.CompilerParams(dimension_semantics=sem, vmem_limit_bytes=vmem_mib * MIB)


def _dot(a, b):
    return jnp.dot(a, b, preferred_element_type=F32)


def _dot_nt(a, b):
    return lax.dot_general(a, b, (((1,), (1,)), ((), ())), preferred_element_type=F32)


def _dot_tn(a, b):
    return lax.dot_general(a, b, (((0,), (0,)), ((), ())), preferred_element_type=F32)


def _rms(x, g):
    ms = jnp.mean(x * x, axis=-1, keepdims=True)
    return x * lax.rsqrt(ms + EPS) * g


def _log_sigmoid(x):
    return jnp.minimum(x, 0.0) - jnp.log1p(jnp.exp(-jnp.abs(x)))


def _sigmoid(x):
    return 1.0 / (1.0 + jnp.exp(-x))


def _in_proj_kernel(x_ref, g_ref, w_ref, wgh_ref, wgl_ref, out_ref, gates_ref, xh_ref):
    j = pl.program_id(1)

    @pl.when(j == 0)
    def _():
        for r in range(0, x_ref.shape[0], ROW_CHUNK):
            rows = slice(r, r + ROW_CHUNK)
            xn = _rms(x_ref[rows, :], g_ref[...])
            xh = xn.astype(BF16)
            xh_ref[rows, :] = xh
            xl = (xn - xh.astype(F32)).astype(BF16)
            gates_ref[rows, :] = _dot(xh, wgh_ref[...]) + _dot(xl, wgh_ref[...]) + _dot(xh, wgl_ref[...])

    out_ref[...] = _dot(xh_ref[...], w_ref[...])


def in_proj(h, gain, w_main, wg_hi, wg_lo, tm=512, tn=1024):
    T, D = h.shape
    N = w_main.shape[1]
    return pl.pallas_call(
        _in_proj_kernel,
        grid=(T // tm, N // tn),
        in_specs=[
            pl.BlockSpec((tm, D), lambda i, j: (i, 0)),
            pl.BlockSpec((1, D), lambda i, j: (0, 0)),
            pl.BlockSpec((D, tn), lambda i, j: (0, j)),
            pl.BlockSpec((D, LANES), lambda i, j: (0, 0)),
            pl.BlockSpec((D, LANES), lambda i, j: (0, 0)),
        ],
        out_specs=[
            pl.BlockSpec((tm, tn), lambda i, j: (i, j)),
            pl.BlockSpec((tm, LANES), lambda i, j: (i, 0)),
        ],
        out_shape=[jax.ShapeDtypeStruct((T, N), F32), jax.ShapeDtypeStruct((T, LANES), F32)],
        scratch_shapes=[pltpu.VMEM((tm, D), BF16)],
        compiler_params=_cparams(("arbitrary", "arbitrary"), 52),
        name="in_proj",
    )(h, gain, w_main, wg_hi, wg_lo)


def _ret_kernel(dl_ref, q_ref, k_ref, v_ref, g_ref, cos_ref, sin_ref, out_ref,
                cb_all, cf, cb, *, L, nc):
    p = pl.program_id(1)
    jj = pl.program_id(2)
    lg = _log_sigmoid(dl_ref[...])
    lgf = lg[0:1, 0:1]
    lgb = lg[1:2, 0:1]
    lcol = lax.broadcasted_iota(jnp.int32, (L, 1), 0).astype(F32)
    cos = cos_ref[...]
    sin = sin_ref[...]
    k = k_ref[...]
    kr = (k * cos + pltpu.roll(k, HEAD_DIM // 2, 1) * sin) * (HEAD_DIM ** -0.5)
    vb = v_ref[...].astype(BF16)

    @pl.when(p == 0)
    def _():
        c = nc - 1 - jj

        @pl.when(jj == 0)
        def _():
            cb[...] = jnp.zeros_like(cb)

        cb_all[c] = cb[...].astype(BF16)
        kw = (kr * jnp.exp(lgb * lcol)).astype(BF16)
        cb[...] = jnp.exp(lgb * L) * cb[...] + _dot_tn(kw, vb)

    @pl.when(p == 1)
    def _():
        c = jj

        @pl.when(jj == 0)
        def _():
            cf[...] = jnp.zeros_like(cf)

        q = q_ref[...]
        qr = q * cos + pltpu.roll(q, HEAD_DIM // 2, 1) * sin
        ii = lax.broadcasted_iota(jnp.int32, (L, L), 0)
        jx = lax.broadcasted_iota(jnp.int32, (L, L), 1)
        diff = (ii - jx).astype(F32)
        dmat = (jnp.where(diff >= 0, jnp.exp(lgf * jnp.maximum(diff, 0.0)), 0.0)
                + jnp.where(diff <= 0, jnp.exp(lgb * jnp.maximum(-diff, 0.0)), 0.0))
        s = _dot_nt(qr.astype(BF16), kr.astype(BF16)) * dmat
        o = _dot(s.astype(BF16), vb)
        o += _dot((qr * jnp.exp(lgf * (lcol + 1.0))).astype(BF16), cf[...].astype(BF16))
        o += _dot((qr * jnp.exp(lgb * (L - lcol))).astype(BF16), cb_all[c])
        kw = (kr * jnp.exp(lgf * (L - 1.0 - lcol))).astype(BF16)
        cf[...] = jnp.exp(lgf * L) * cf[...] + _dot_tn(kw, vb)
        mu = jnp.mean(o, axis=-1, keepdims=True)
        oc = o - mu
        var = jnp.mean(oc * oc, axis=-1, keepdims=True)
        on = oc * lax.rsqrt(var + EPS)
        g = g_ref[...]
        out_ref[...] = g * _sigmoid(g) * on


def retention(proj, decay_logit, cos_t, sin_t, L=RET_L):
    T = proj.shape[0]
    nc = T // L
    dl = jnp.broadcast_to(decay_logit.astype(F32).T[:, :, None], (N_HEADS, 2, LANES))

    def chunk(p, jj):
        return p * jj + (1 - p) * (nc - 1 - jj)

    return pl.pallas_call(
        functools.partial(_ret_kernel, L=L, nc=nc),
        grid=(N_HEADS, 2, nc),
        in_specs=[
            pl.BlockSpec((None, 2, LANES), lambda h, p, jj: (h, 0, 0)),
            pl.BlockSpec((L, HEAD_DIM), lambda h, p, jj: (p * jj, CB_RQ + h)),
            pl.BlockSpec((L, HEAD_DIM), lambda h, p, jj: (chunk(p, jj), CB_RK + h)),
            pl.BlockSpec((L, HEAD_DIM), lambda h, p, jj: (chunk(p, jj), CB_RV + h)),
            pl.BlockSpec((L, HEAD_DIM), lambda h, p, jj: (p * jj, CB_RG + h)),
            pl.BlockSpec((L, HEAD_DIM), lambda h, p, jj: (chunk(p, jj), 0)),
            pl.BlockSpec((L, HEAD_DIM), lambda h, p, jj: (chunk(p, jj), 0)),
        ],
        out_specs=pl.BlockSpec((L, HEAD_DIM), lambda h, p, jj: (p * jj, h)),
        out_shape=jax.ShapeDtypeStruct((T, GROUP_W), F32),
        scratch_shapes=[
            pltpu.VMEM((nc, HEAD_DIM, HEAD_DIM), BF16),
            pltpu.VMEM((HEAD_DIM, HEAD_DIM), F32),
            pltpu.VMEM((HEAD_DIM, HEAD_DIM), F32),
        ],
        compiler_params=_cparams(("arbitrary", "arbitrary", "arbitrary"), 32),
        name="retention",
    )(dl, proj, proj, proj, proj, cos_t, sin_t)


def _ml_conv(x, prev_row, next_row, w_ref, b_ref, L):
    ri = lax.broadcasted_iota(jnp.int32, (L, HEAD_DIM), 0)
    xm = jnp.where(ri == 0, prev_row, pltpu.roll(x, 1, 0))
    xp = jnp.where(ri == L - 1, next_row, pltpu.roll(x, L - 1, 0))
    y = b_ref[...] + xm * w_ref[0:1, :] + x * w_ref[1:2, :] + xp * w_ref[2:3, :]
    return y * _sigmoid(y)


def _ml_kernel(gt_ref, gb_ref, q_ref, qp_ref, qn_ref, k_ref, kp_ref, kn_ref, v_ref, o_ref,
               wq_ref, bq_ref, wk_ref, bk_ref, out_ref,
               cb_all, nb_all, mb_all, c_st, n_st, m_st, *, L, nc):
    p = pl.program_id(1)
    jj = pl.program_id(2)
    c = p * jj + (1 - p) * (nc - 1 - jj)
    has_prev = (c > 0).astype(F32)
    has_next = (c < nc - 1).astype(F32)

    kc = _ml_conv(k_ref[...], kp_ref[7:8, :] * has_prev, kn_ref[0:1, :] * has_next, wk_ref, bk_ref, L)
    kc = kc * (HEAD_DIM ** -0.5)
    kb = kc.astype(BF16)
    vb = v_ref[...].astype(BF16)

    pre = gt_ref[...] + gb_ref[...]
    ii = lax.broadcasted_iota(jnp.int32, (L, L), 0)
    jx = lax.broadcasted_iota(jnp.int32, (L, L), 1)
    eye = ii == jx

    def to_row(col):
        return jnp.sum(jnp.where(eye, col, 0.0), axis=0, keepdims=True)

    def state_update(a_col, g_tot):
        m_prev = m_st[0:1, 0:1]
        m_new = jnp.maximum(g_tot + m_prev, jnp.max(a_col, axis=0, keepdims=True))
        decay = jnp.exp(g_tot + m_prev - m_new)
        wk = jnp.exp(a_col - m_new)
        kwf = kc * wk
        c_st[...] = decay * c_st[...] + _dot_tn(kwf.astype(BF16), vb)
        n_st[...] = decay * n_st[...] + jnp.sum(kwf, axis=0, keepdims=True)
        m_st[...] = jnp.broadcast_to(m_new, m_st.shape)

    @pl.when(jj == 0)
    def _():
        c_st[...] = jnp.zeros_like(c_st)
        n_st[...] = jnp.zeros_like(n_st)
        m_st[...] = jnp.full(m_st.shape, M_INIT, F32)

    @pl.when(p == 0)
    def _():
        cb_all[c] = c_st[...].astype(BF16)
        nb_all[pl.ds(c, 1), :] = n_st[...]
        mb_all[pl.ds(c, 1), :] = m_st[...]
        i_b = pre[:, 2:3]
        lf_b = _log_sigmoid(pre[:, 3:4])
        lf_b_row = to_row(lf_b)
        rb_col = jnp.sum(jnp.where(jx >= ii, lf_b_row, 0.0), axis=1, keepdims=True)
        g_tot = rb_col[0:1, :]
        state_update(g_tot - rb_col + i_b, g_tot)

    @pl.when(p == 1)
    def _():
        qc = _ml_conv(q_ref[...], qp_ref[7:8, :] * has_prev, qn_ref[0:1, :] * has_next, wq_ref, bq_ref, L)
        qb = qc.astype(BF16)
        qk = _dot_nt(qb, kb)

        def direction(d_log, inter_log, c_prev, n_prev):
            rowmax = jnp.max(d_log, axis=1, keepdims=True)
            m = jnp.maximum(inter_log, rowmax)
            w_intra = jnp.exp(d_log - m)
            w_inter = jnp.exp(inter_log - m)
            s = qk * w_intra
            num = _dot(s.astype(BF16), vb) + w_inter * _dot(qb, c_prev)
            qn = jnp.sum(s, axis=1, keepdims=True) + w_inter * jnp.sum(qc * n_prev, axis=1, keepdims=True)
            den = jnp.maximum(jnp.abs(qn), jnp.exp(-m))
            return num * (1.0 / den)

        i_f = pre[:, 0:1]
        lf_f = _log_sigmoid(pre[:, 1:2])
        lf_f_row = to_row(lf_f)
        i_f_row = to_row(i_f)
        b_col = jnp.sum(jnp.where(jx <= ii, lf_f_row, 0.0), axis=1, keepdims=True)
        b_row = jnp.sum(jnp.where(ii <= jx, lf_f, 0.0), axis=0, keepdims=True)
        d_log_f = jnp.where(jx <= ii, b_col - b_row + i_f_row, NEG)
        h_f = direction(d_log_f, b_col + m_st[0:1, 0:1], c_st[...].astype(BF16), n_st[...])

        i_b = pre[:, 2:3]
        lf_b = _log_sigmoid(pre[:, 3:4])
        lf_b_row = to_row(lf_b)
        i_b_row = to_row(i_b)
        rb_col = jnp.sum(jnp.where(jx >= ii, lf_b_row, 0.0), axis=1, keepdims=True)
        rb_row = jnp.sum(jnp.where(ii >= jx, lf_b, 0.0), axis=0, keepdims=True)
        d_log_b = jnp.where(jx >= ii, rb_col - rb_row + i_b_row, NEG)
        m_next = mb_all[pl.ds(c, 1), :][:, 0:1]
        h_b = direction(d_log_b, rb_col + m_next, cb_all[c], nb_all[pl.ds(c, 1), :])

        og = o_ref[...]
        out_ref[...] = _sigmoid(og) * (h_f + h_b)

        g_tot = b_col[L - 1:L, :]
        state_update(g_tot - b_col + i_f, g_tot)


def mlstm(proj, gates, gate_bias, conv_w, conv_b, L=ML_L):
    T = proj.shape[0]
    nc = T // L
    hb = L // 8
    n8 = T // 8
    gt = gates[:, :N_GATE].reshape(T, 4, N_HEADS).transpose(2, 0, 1)
    gb = gate_bias.astype(F32).T.reshape(N_HEADS, 1, 4)
    cw = conv_w.astype(F32)
    cb = conv_b.astype(F32).reshape(1, 2 * GROUP_W)

    def chunk(p, jj):
        return p * jj + (1 - p) * (nc - 1 - jj)

    def prev8(p, jj):
        return jnp.maximum(chunk(p, jj) * hb - 1, 0)

    def next8(p, jj):
        return jnp.minimum((chunk(p, jj) + 1) * hb, n8 - 1)

    blk = (L, HEAD_DIM)
    halo = (8, HEAD_DIM)
    return pl.pallas_call(
        functools.partial(_ml_kernel, L=L, nc=nc),
        grid=(N_HEADS, 2, nc),
        in_specs=[
            pl.BlockSpec((None, L, 4), lambda h, p, jj: (h, chunk(p, jj), 0)),
            pl.BlockSpec((None, 1, 4), lambda h, p, jj: (h, 0, 0)),
            pl.BlockSpec(blk, lambda h, p, jj: (p * jj, CB_MQ + h)),
            pl.BlockSpec(halo, lambda h, p, jj: (p * prev8(p, jj), CB_MQ + h)),
            pl.BlockSpec(halo, lambda h, p, jj: (p * next8(p, jj), CB_MQ + h)),
            pl.BlockSpec(blk, lambda h, p, jj: (chunk(p, jj), CB_MK + h)),
            pl.BlockSpec(halo, lambda h, p, jj: (prev8(p, jj), CB_MK + h)),
            pl.BlockSpec(halo, lambda h, p, jj: (next8(p, jj), CB_MK + h)),
            pl.BlockSpec(blk, lambda h, p, jj: (chunk(p, jj), CB_MV + h)),
            pl.BlockSpec(blk, lambda h, p, jj: (p * jj, CB_MO + h)),
            pl.BlockSpec((3, HEAD_DIM), lambda h, p, jj: (0, h)),
            pl.BlockSpec((1, HEAD_DIM), lambda h, p, jj: (0, h)),
            pl.BlockSpec((3, HEAD_DIM), lambda h, p, jj: (0, N_HEADS + h)),
            pl.BlockSpec((1, HEAD_DIM), lambda h, p, jj: (0, N_HEADS + h)),
        ],
        out_specs=pl.BlockSpec(blk, lambda h, p, jj: (p * jj, h)),
        out_shape=jax.ShapeDtypeStruct((T, GROUP_W), F32),
        scratch_shapes=[
            pltpu.VMEM((nc, HEAD_DIM, HEAD_DIM), BF16),
            pltpu.VMEM((nc, HEAD_DIM), F32),
            pltpu.VMEM((nc, HEAD_DIM), F32),
            pltpu.VMEM((HEAD_DIM, HEAD_DIM), F32),
            pltpu.VMEM((1, HEAD_DIM), F32),
            pltpu.VMEM((1, HEAD_DIM), F32),
        ],
        compiler_params=_cparams(("arbitrary", "arbitrary", "arbitrary"), 32),
        name="mlstm",
    )(gt, gb, proj, proj, proj, proj, proj, proj, proj, proj, cw, cb, cw, cb)


def s5_tables(lam_re, lam_im, log_step, b_re, b_im, c_re, c_im, Lc=S5_LC):
    f = lambda t: t.astype(F32)
    lam_re, lam_im, log_step, b_re, b_im, c_re, c_im = map(f, (lam_re, lam_im, log_step, b_re, b_im, c_re, c_im))
    G, P, C = S5_GROUPS, S5_STATE, S5_CH
    step = jnp.exp(log_step)[..., None]
    lre = jnp.minimum(lam_re, LAMBDA_RE_MAX)
    mag = jnp.exp(lre * step)
    ang = lam_im * step
    abar_re, abar_im = mag * jnp.cos(ang), mag * jnp.sin(ang)
    den = lre * lre + lam_im * lam_im
    zr, zi = abar_re - 1.0, abar_im
    coef_re = (zr * lre + zi * lam_im) / den
    coef_im = (zi * lre - zr * lam_im) / den
    bbar_re = coef_re[..., None] * b_re - coef_im[..., None] * b_im
    bbar_im = coef_re[..., None] * b_im + coef_im[..., None] * b_re
    n = jnp.arange(Lc + 1, dtype=F32)
    pmag = jnp.exp((lre * step)[..., None] * n)
    pang = ang[..., None] * n
    pw_re, pw_im = pmag * jnp.cos(pang), pmag * jnp.sin(pang)
    ct_re = jnp.swapaxes(c_re, -1, -2)[..., None]
    ct_im = jnp.swapaxes(c_im, -1, -2)[..., None]
    br_, bi_ = bbar_re[..., None, :], bbar_im[..., None, :]
    cb_re = ct_re * br_ - ct_im * bi_
    cb_im = ct_re * bi_ + ct_im * br_
    hp = lax.Precision.HIGHEST
    kk = (jnp.einsum('xgpn,xgpce->xgnce', pw_re[..., :Lc], cb_re, precision=hp)
          - jnp.einsum('xgpn,xgpce->xgnce', pw_im[..., :Lc], cb_im, precision=hp))
    s_i = np.arange(Lc)[:, None]
    t_i = np.arange(Lc)[None, :]
    lag = t_i - s_i
    kf = jnp.where((lag >= 0)[None, :, :, None, None], kk[0][:, np.maximum(lag, 0)], 0.0)
    kb = jnp.where((lag <= 0)[None, :, :, None, None], kk[1][:, np.maximum(-lag, 0)], 0.0)
    mt = (kf + kb).transpose(0, 1, 4, 2, 3).reshape(G, Lc * C, Lc * C).astype(BF16)
    def summ(pr, pi, br, bi):
        pr, pi = (jnp.swapaxes(t, 1, 2)[:, :, None, :] for t in (pr, pi))
        br, bi = (jnp.swapaxes(t, 1, 2)[:, None, :, :] for t in (br, bi))
        return pr * br - pi * bi, pr * bi + pi * br
    f_re, f_im = summ(pw_re[0][..., Lc - 1::-1][..., :Lc], pw_im[0][..., Lc - 1::-1][..., :Lc], bbar_re[0], bbar_im[0])
    r_re, r_im = summ(pw_re[1][..., :Lc], pw_im[1][..., :Lc], bbar_re[1], bbar_im[1])
    wb = jnp.concatenate([f_re, f_im, r_re, r_im], axis=-1).reshape(G, Lc * C, 4 * P).astype(BF16)
    def read(pr, pi, cr, ci):
        pr, pi = pr[:, :, :, None], pi[:, :, :, None]
        cr, ci = (jnp.swapaxes(t, 1, 2)[:, :, None, :] for t in (cr, ci))
        return cr * pr - ci * pi, -(cr * pi + ci * pr)
    o_re, o_im = read(pw_re[0][..., 1:], pw_im[0][..., 1:], c_re[0], c_im[0])
    q_re, q_im = read(pw_re[1][..., Lc:0:-1], pw_im[1][..., Lc:0:-1], c_re[1], c_im[1])
    wc = jnp.concatenate([o_re, o_im, q_re, q_im], axis=1).reshape(G, 4 * P, Lc * C).astype(BF16)
    al_re, al_im = pw_re[..., Lc], pw_im[..., Lc]
    a1 = jnp.concatenate([al_re, al_re], axis=-1)
    a2 = jnp.concatenate([-al_im, al_im], axis=-1)
    return mt, wb, wc, a1, a2


def _s5_summary_kernel(x_ref, wb_ref, e_ref):
    e_ref[...] = _dot(x_ref[...], wb_ref[...])


def _s5_scan_kernel(ef_ref, eb_ref, a1_ref, a2_ref, sf_ref, sb_ref, *, nc):
    a1f, a2f = a1_ref[0], a2_ref[0]
    a1b, a2b = a1_ref[1], a2_ref[1]
    half = S5_STATE

    def body(j, carry):
        sf, sb = carry
        jb = nc - 1 - j
        sf_ref[j] = sf
        sb_ref[jb] = sb
        sf = a1f * sf + a2f * pltpu.roll(sf, half, 1) + ef_ref[j]
        sb = a1b * sb + a2b * pltpu.roll(sb, half, 1) + eb_ref[jb]
        return sf, sb

    z = jnp.zeros(a1f.shape, F32)
    lax.fori_loop(0, nc, body, (z, z))


def _s5_out_kernel(x_ref, mt_ref, s_ref, wc_ref, y_ref):
    y_ref[...] = _dot(x_ref[...], mt_ref[...]) + _dot(s_ref[...], wc_ref[...])


def s5_core(u_main, tables, Lc=S5_LC):
    mt, wb, wc, a1, a2 = tables
    T = u_main.shape[0]
    nc = T // Lc
    G, C, P = S5_GROUPS, S5_CH, S5_STATE
    K = Lc * C
    x = u_main.astype(BF16).reshape(nc, Lc, G, C).transpose(2, 0, 1, 3).reshape(G, nc, K)
    e = pl.pallas_call(
        _s5_summary_kernel,
        grid=(G,),
        in_specs=[pl.BlockSpec((None, nc, K), lambda g: (g, 0, 0)),
                  pl.BlockSpec((None, K, 4 * P), lambda g: (g, 0, 0))],
        out_specs=pl.BlockSpec((None, nc, 4 * P), lambda g: (g, 0, 0)),
        out_shape=jax.ShapeDtypeStruct((G, nc, 4 * P), F32),
        compiler_params=_cparams(("arbitrary",), 32),
        name="s5_summary",
    )(x, wb)
    et = e.transpose(1, 0, 2)
    ef, eb = et[..., :2 * P], et[..., 2 * P:]
    gs = 16
    sf, sb = pl.pallas_call(
        functools.partial(_s5_scan_kernel, nc=nc),
        grid=(G // gs,),
        in_specs=[pl.BlockSpec((nc, gs, 2 * P), lambda i: (0, i, 0)),
                  pl.BlockSpec((nc, gs, 2 * P), lambda i: (0, i, 0)),
                  pl.BlockSpec((2, gs, 2 * P), lambda i: (0, i, 0)),
                  pl.BlockSpec((2, gs, 2 * P), lambda i: (0, i, 0))],
        out_specs=[pl.BlockSpec((nc, gs, 2 * P), lambda i: (0, i, 0)),
                   pl.BlockSpec((nc, gs, 2 * P), lambda i: (0, i, 0))],
        out_shape=[jax.ShapeDtypeStruct((nc, G, 2 * P), F32)] * 2,
        compiler_params=_cparams(("arbitrary",), 32),
        name="s5_scan",
    )(ef, eb, a1, a2)
    s = jnp.concatenate([sf, sb], axis=-1).transpose(1, 0, 2).astype(BF16)
    y = pl.pallas_call(
        _s5_out_kernel,
        grid=(G,),
        in_specs=[pl.BlockSpec((None, nc, K), lambda g: (g, 0, 0)),
                  pl.BlockSpec((None, K, K), lambda g: (g, 0, 0)),
                  pl.BlockSpec((None, nc, 4 * P), lambda g: (g, 0, 0)),
                  pl.BlockSpec((None, 4 * P, K), lambda g: (g, 0, 0))],
        out_specs=pl.BlockSpec((None, nc, K), lambda g: (g, 0, 0)),
        out_shape=jax.ShapeDtypeStruct((G, nc, K), F32),
        compiler_params=_cparams(("arbitrary",), 32),
        name="s5_out",
    )(x, mt, s, wc)
    return y.reshape(G, nc, Lc, C).transpose(1, 2, 0, 3).reshape(T, GROUP_W)


def _s5_post_kernel(y_ref, u_ref, d_ref, w_ref, b_ref, out_ref):
    y = y_ref[...] + d_ref[...] * u_ref[...]
    y = jax.nn.gelu(y)
    z = _dot(y.astype(BF16), w_ref[...]) + b_ref[...]
    out_ref[...] = y * _sigmoid(z)


def s5_post(y, proj, d_skip, glu_w, glu_b, tm=512):
    T = y.shape[0]
    return pl.pallas_call(
        _s5_post_kernel,
        grid=(T // tm,),
        in_specs=[pl.BlockSpec((tm, GROUP_W), lambda i: (i, 0)),
                  pl.BlockSpec((tm, GROUP_W), lambda i: (i, CB_SU // 8)),
                  pl.BlockSpec((1, GROUP_W), lambda i: (0, 0)),
                  pl.BlockSpec((GROUP_W, GROUP_W), lambda i: (0, 0)),
                  pl.BlockSpec((1, GROUP_W), lambda i: (0, 0))],
        out_specs=pl.BlockSpec((tm, GROUP_W), lambda i: (i, 0)),
        out_shape=jax.ShapeDtypeStruct((T, GROUP_W), F32),
        compiler_params=_cparams(("arbitrary",), 32),
        name="s5_post",
    )(y, proj, d_skip.astype(F32).reshape(1, GROUP_W), glu_w.astype(BF16), glu_b.astype(F32).reshape(1, GROUP_W))


def na_bias_tables(rel_bias):
    rows = 4 * NA_QR
    kh, kw = NA_KH, NA_KW
    out = []
    for r0 in (0, NA_QR, rows - NA_QR):
        qr = r0 + np.arange(NA_QR)[:, None, None, None]
        qc = np.arange(GRID_W)[None, :, None, None]
        kr = r0 - NA_KR + np.arange(4 * NA_KR)[None, None, :, None]
        kc = np.arange(GRID_W)[None, None, None, :]
        rs = np.clip(qr - kh // 2, 0, rows - kh)
        cs = np.clip(qc - kw // 2, 0, GRID_W - kw)
        valid = (kr >= 0) & (kr < rows) & (kr >= rs) & (kr < rs + kh) & (kc >= cs) & (kc < cs + kw)
        dr = np.clip(kr - qr + (NA_KH - 1), 0, 2 * NA_KH - 2)
        dc = np.clip(kc - qc + (NA_KW - 1), 0, 2 * NA_KW - 2)
        shape = (NA_QR, GRID_W, 4 * NA_KR, GRID_W)
        valid = np.broadcast_to(valid, shape).reshape(NA_QR * GRID_W, 4 * NA_KR * GRID_W)
        dr = np.broadcast_to(dr, shape).reshape(valid.shape)
        dc = np.broadcast_to(dc, shape).reshape(valid.shape)
        b = rel_bias.astype(F32)[:, dr, dc]
        out.append(jnp.where(valid[None], b, NEG))
    return jnp.stack(out, axis=1)


def _na_kernel(q_ref, k0, k1, k2, k3, v0, v1, v2, v3, tab_ref, out_ref, *, kb):
    q = (q_ref[...] * (HEAD_DIM ** -0.5)).astype(BF16)
    ks = (k0, k1, k2, k3)
    vs = (v0, v1, v2, v3)
    s = jnp.concatenate([_dot_nt(q, kr[...].astype(BF16)) for kr in ks], axis=1) + tab_ref[...]
    m = jnp.max(s, axis=1, keepdims=True)
    p = jnp.exp(s - m)
    l = jnp.sum(p, axis=1, keepdims=True)
    pb = p.astype(BF16)
    o = _dot(pb[:, 0:kb], vs[0][...].astype(BF16))
    for i in range(1, 4):
        o += _dot(pb[:, i * kb:(i + 1) * kb], vs[i][...].astype(BF16))
    out_ref[...] = o * (1.0 / l)


def neighbourhood_attention(proj, tables):
    T = proj.shape[0]
    rows = T // GRID_W
    assert rows >= 3 * NA_QR and rows % NA_QR == 0
    tq = NA_QR * GRID_W
    kb = NA_KR * GRID_W
    nb = T // tq
    nkb = T // kb

    def kidx(b, i):
        return jnp.clip(2 * b - 1 + i, 0, nkb - 1)

    def pat(b):
        return jnp.where(b == 0, 0, jnp.where(b == nb - 1, 2, 1))

    kspecs = [pl.BlockSpec((kb, HEAD_DIM), functools.partial(lambda h, b, i: (kidx(b, i), CB_NK + h), i=i))
              for i in range(4)]
    vspecs = [pl.BlockSpec((kb, HEAD_DIM), functools.partial(lambda h, b, i: (kidx(b, i), CB_NV + h), i=i))
              for i in range(4)]
    return pl.pallas_call(
        functools.partial(_na_kernel, kb=kb),
        grid=(N_HEADS, nb),
        in_specs=[pl.BlockSpec((tq, HEAD_DIM), lambda h, b: (b, CB_NQ + h))] + kspecs + vspecs + [
            pl.BlockSpec((None, None, tq, 4 * kb), lambda h, b: (h, pat(b), 0, 0))],
        out_specs=pl.BlockSpec((tq, HEAD_DIM), lambda h, b: (b, h)),
        out_shape=jax.ShapeDtypeStruct((T, GROUP_W), F32),
        compiler_params=_cparams(("arbitrary", "arbitrary"), 40),
        name="neighbourhood_attention",
    )(proj, *([proj] * 8), tables)


def _out_proj_kernel(a0, a1, a2, a3, g_ref, w_ref, h_ref, out_ref, m_ref):
    j = pl.program_id(1)

    @pl.when(j == 0)
    def _():
        for gi, a in enumerate((a0, a1, a2, a3)):
            for r in range(0, a.shape[0], ROW_CHUNK):
                rows = slice(r, r + ROW_CHUNK)
                m_ref[rows, gi * GROUP_W:(gi + 1) * GROUP_W] = _rms(a[rows, :], g_ref[gi:gi + 1, :]).astype(BF16)

    out_ref[...] = h_ref[...] + _dot(m_ref[...], w_ref[...])


def out_proj(groups, gain, w_out, h, tm=512, tn=1024):
    T, D = h.shape
    gspec = pl.BlockSpec((tm, GROUP_W), lambda i, j: (i, 0))
    return pl.pallas_call(
        _out_proj_kernel,
        grid=(T // tm, D // tn),
        in_specs=[gspec, gspec, gspec, gspec,
                  pl.BlockSpec((4, GROUP_W), lambda i, j: (0, 0)),
                  pl.BlockSpec((D, tn), lambda i, j: (0, j)),
                  pl.BlockSpec((tm, tn), lambda i, j: (i, j))],
        out_specs=pl.BlockSpec((tm, tn), lambda i, j: (i, j)),
        out_shape=jax.ShapeDtypeStruct((T, D), F32),
        scratch_shapes=[pltpu.VMEM((tm, D), BF16)],
        compiler_params=_cparams(("arbitrary", "arbitrary"), 52),
        name="out_proj",
    )(*groups, gain, w_out, h)


def _router_kernel(h_ref, g_ref, wr_ref, xn_ref, aff_ref):
    for r in range(0, h_ref.shape[0], ROW_CHUNK):
        rows = slice(r, r + ROW_CHUNK)
        xn = _rms(h_ref[rows, :], g_ref[...])
        xn_ref[rows, :] = xn
        logits = jnp.dot(xn, wr_ref[...], precision=lax.Precision.HIGHEST, preferred_element_type=F32)
        lane = lax.broadcasted_iota(jnp.int32, logits.shape, 1)
        logits = jnp.where(lane < N_EXPERTS, logits, NEG)
        m = jnp.max(logits, axis=1, keepdims=True)
        e = jnp.exp(logits - m)
        aff_ref[rows, :] = e / jnp.sum(e, axis=1, keepdims=True)


def router(h, gain, w_router_pad, tm=512):
    T, D = h.shape
    return pl.pallas_call(
        _router_kernel,
        grid=(T // tm,),
        in_specs=[pl.BlockSpec((tm, D), lambda i: (i, 0)),
                  pl.BlockSpec((1, D), lambda i: (0, 0)),
                  pl.BlockSpec((D, LANES), lambda i: (0, 0))],
        out_specs=[pl.BlockSpec((tm, D), lambda i: (i, 0)),
                   pl.BlockSpec((tm, LANES), lambda i: (i, 0))],
        out_shape=[jax.ShapeDtypeStruct((T, D), F32), jax.ShapeDtypeStruct((T, LANES), F32)],
        compiler_params=_cparams(("arbitrary",), 48),
        name="router",
    )(h, gain, w_router_pad)


def _moe_up_kernel(idx_ref, x_hbm, wg_ref, wu_ref, hid_ref, xbuf, xb16, sem, *, tm, n_tiles):
    e = pl.program_id(0)
    m = pl.program_id(1)
    f = pl.program_id(2)
    n = e * pl.num_programs(1) + m
    slot = n % 2

    def row_copy(tok, r, s):
        return pltpu.make_async_copy(x_hbm.at[pl.ds(tok, 1), :], xbuf.at[s, pl.ds(r, 1), :], sem.at[s])

    def issue(tile, s):
        base = tile * tm

        def body(r, c):
            row_copy(idx_ref[base + r], r, s).start()
            return c

        lax.fori_loop(0, tm, body, 0, unroll=8)

    def wait(s):
        def body(r, c):
            row_copy(0, r, s).wait()
            return c

        lax.fori_loop(0, tm, body, 0, unroll=8)

    @pl.when(f == 0)
    def _():
        @pl.when(n == 0)
        def _():
            issue(0, 0)

        @pl.when(n + 1 < n_tiles)
        def _():
            issue(n + 1, 1 - slot)

        wait(slot)
        xb16[...] = xbuf[slot].astype(BF16)

    x = xb16[...]
    a = _dot(x, wg_ref[...])
    b = _dot(x, wu_ref[...])
    hid_ref[...] = (a * _sigmoid(a) * b).astype(BF16)


def moe_up(idx_flat, xn, wg, wu, cap, tm=MOE_TM, tf=512):
    E, D, F = wg.shape
    n_m = cap // tm
    grid_spec = pltpu.PrefetchScalarGridSpec(
        num_scalar_prefetch=1,
        grid=(E, n_m, F // tf),
        in_specs=[pl.BlockSpec(memory_space=pl.ANY),
                  pl.BlockSpec((None, D, tf), lambda e, m, f, idx: (e, 0, f)),
                  pl.BlockSpec((None, D, tf), lambda e, m, f, idx: (e, 0, f))],
        out_specs=pl.BlockSpec((None, tm, tf), lambda e, m, f, idx: (e, m, f)),
        scratch_shapes=[pltpu.VMEM((2, tm, D), F32), pltpu.VMEM((tm, D), BF16), pltpu.SemaphoreType.DMA((2,))],
    )
    return pl.pallas_call(
        functools.partial(_moe_up_kernel, tm=tm, n_tiles=E * n_m),
        grid_spec=grid_spec,
        out_shape=jax.ShapeDtypeStruct((E, cap, F), BF16),
        compiler_params=_cparams(("arbitrary", "arbitrary", "arbitrary"), 48),
        name="moe_up",
    )(idx_flat, xn, wg, wu)


def _moe_down_kernel(hid_ref, wd_ref, gate_ref, ye_ref):
    ye_ref[...] = (_dot(hid_ref[...], wd_ref[...]) * gate_ref[...]).astype(BF16)


def moe_down(hid, wd, gate, tm=1024, tn=1024):
    E, cap, F = hid.shape
    D = wd.shape[2]
    tm = min(tm, cap)
    return pl.pallas_call(
        _moe_down_kernel,
        grid=(E, cap // tm, D // tn),
        in_specs=[pl.BlockSpec((None, tm, F), lambda e, m, n: (e, m, 0)),
                  pl.BlockSpec((None, F, tn), lambda e, m, n: (e, 0, n)),
                  pl.BlockSpec((None, tm, 1), lambda e, m, n: (e, m, 0))],
        out_specs=pl.BlockSpec((None, tm, tn), lambda e, m, n: (e, m, n)),
        out_shape=jax.ShapeDtypeStruct((E, cap, D), BF16),
        compiler_params=_cparams(("arbitrary", "arbitrary", "arbitrary"), 40),
        name="moe_down",
    )(hid, wd, gate)


def _combine_kernel(st_ref, h_ref, ia_ref, ib_ref, ya_ref, yb_ref, idx_hbm, ye_hbm, out_ref,
                    ybuf, ibuf, sem, *, tb, W, nbp1, n_win, cn):
    b = pl.program_id(0)
    e = pl.program_id(1)
    D = out_ref.shape[1]

    @pl.when(e == 0)
    def _():
        out_ref[...] = h_ref[...]

    s0 = st_ref[e * nbp1 + b]
    s1 = st_ref[e * nbp1 + b + 1]
    wa = s0 // W
    tok = b * tb + lax.broadcasted_iota(jnp.int32, (tb, 1), 0)

    def onehot(idx_row, valid):
        return jnp.where(tok == idx_row * valid + (valid - 1), 1.0, 0.0).astype(BF16)

    def accumulate(oh, y_ref):
        for c0 in range(0, D, cn):
            out_ref[:, c0:c0 + cn] += _dot(oh, y_ref[:, c0:c0 + cn])

    accumulate(onehot(ia_ref[...], (wa < n_win).astype(jnp.int32)), ya_ref)
    accumulate(onehot(ib_ref[...], (wa + 1 < n_win).astype(jnp.int32)), yb_ref)

    need = (s1 + W - 1) // W - wa

    @pl.when(need > 2)
    def _():
        def body(w, c):
            off = pl.multiple_of((wa + w) * W, W)
            cy = pltpu.make_async_copy(ye_hbm.at[e, pl.ds(off, W), :], ybuf, sem.at[0])
            ci = pltpu.make_async_copy(idx_hbm.at[e, :, pl.ds(off, W)], ibuf, sem.at[1])
            cy.start()
            ci.start()
            cy.wait()
            ci.wait()
            accumulate(onehot(ibuf[...], 1), ybuf)
            return c

        lax.fori_loop(2, need, body, 0)


def moe_combine(starts_flat, h, idx_s, ye, tb=COMB_TB, W=COMB_W):
    T, D = h.shape
    E, cap, _ = ye.shape
    nb = T // tb
    n_win = cap // W
    idx4 = idx_s.reshape(E, n_win, 1, W)
    idx3 = idx_s.reshape(E, 1, cap)

    def win(b, e, st, k):
        return jnp.minimum(st[e * (nb + 1) + b] // W + k, n_win - 1)

    grid_spec = pltpu.PrefetchScalarGridSpec(
        num_scalar_prefetch=1,
        grid=(nb, E),
        in_specs=[pl.BlockSpec((tb, D), lambda b, e, st: (b, 0)),
                  pl.BlockSpec((None, None, 1, W), lambda b, e, st: (e, win(b, e, st, 0), 0, 0)),
                  pl.BlockSpec((None, None, 1, W), lambda b, e, st: (e, win(b, e, st, 1), 0, 0)),
                  pl.BlockSpec((None, W, D), lambda b, e, st: (e, win(b, e, st, 0), 0)),
                  pl.BlockSpec((None, W, D), lambda b, e, st: (e, win(b, e, st, 1), 0)),
                  pl.BlockSpec(memory_space=pl.ANY),
                  pl.BlockSpec(memory_space=pl.ANY)],
        out_specs=pl.BlockSpec((tb, D), lambda b, e, st: (b, 0)),
        scratch_shapes=[pltpu.VMEM((W, D), BF16), pltpu.VMEM((1, W), jnp.int32), pltpu.SemaphoreType.DMA((2,))],
    )
    return pl.pallas_call(
        functools.partial(_combine_kernel, tb=tb, W=W, nbp1=nb + 1, n_win=n_win, cn=1024),
        grid_spec=grid_spec,
        out_shape=jax.ShapeDtypeStruct((T, D), F32),
        compiler_params=_cparams(("arbitrary", "arbitrary"), 48),
        name="moe_combine",
    )(starts_flat, h, idx4, idx4, ye, ye, idx3, ye)


def expert_choice_ffn(h, gain, w_router_pad, wg, wu, wd):
    T, D = h.shape
    E = N_EXPERTS
    cap = EC_CAPACITY * T // E
    xn, aff = router(h, gain, w_router_pad)
    gate, idx = lax.top_k(aff[:, :E].T, cap)
    order = jnp.argsort(idx, axis=-1)
    idx_s = jnp.take_along_axis(idx, order, axis=-1).astype(jnp.int32)
    gate_s = jnp.take_along_axis(gate, order, axis=-1)
    hid = moe_up(idx_s.reshape(-1), xn, wg, wu, cap)
    ye = moe_down(hid, wd, gate_s[..., None])
    nb = T // COMB_TB
    edges = (jnp.arange(nb + 1, dtype=jnp.int32) * COMB_TB)
    starts = jnp.sum(idx_s[:, :, None] < edges[None, None, :], axis=1).astype(jnp.int32)
    return moe_combine(starts.reshape(-1), h, idx_s, ye)


def _ple_kernel(h_ref, g_ref, p_ref, wp_ref, wg_ref, gf_ref, out_ref, hn_ref, *, tn, final):
    j = pl.program_id(1)

    tm = h_ref.shape[0]

    @pl.when(j == 0)
    def _():
        for r in range(0, tm, ROW_CHUNK):
            rows = slice(r, r + ROW_CHUNK)
            hn_ref[rows, :] = _rms(h_ref[rows, :], g_ref[...]).astype(BF16)

    gate = _sigmoid(_dot(hn_ref[...], wg_ref[...]))
    emb = _dot(p_ref[...].astype(BF16), wp_ref[...])
    c0 = pl.multiple_of(j * tn, tn)
    out_ref[:, pl.ds(c0, tn)] = h_ref[:, pl.ds(c0, tn)] + emb * gate

    if final:
        @pl.when(j == pl.num_programs(1) - 1)
        def _():
            for r in range(0, tm, ROW_CHUNK):
                rows = slice(r, r + ROW_CHUNK)
                out_ref[rows, :] = _rms(out_ref[rows, :], gf_ref[...])


def ple(h, gain, p, w_ple, w_gate, gain_final, final, tm=512, tn=512):
    T, D = h.shape
    return pl.pallas_call(
        functools.partial(_ple_kernel, tn=tn, final=final),
        grid=(T // tm, D // tn),
        in_specs=[pl.BlockSpec((tm, D), lambda i, j: (i, 0)),
                  pl.BlockSpec((1, D), lambda i, j: (0, 0)),
                  pl.BlockSpec((tm, PLE_DIM), lambda i, j: (i, 0)),
                  pl.BlockSpec((PLE_DIM, tn), lambda i, j: (0, j)),
                  pl.BlockSpec((D, tn), lambda i, j: (0, j)),
                  pl.BlockSpec((1, D), lambda i, j: (0, 0))],
        out_specs=pl.BlockSpec((tm, D), lambda i, j: (i, 0)),
        out_shape=jax.ShapeDtypeStruct((T, D), F32),
        scratch_shapes=[pltpu.VMEM((tm, D), BF16)],
        compiler_params=_cparams(("arbitrary", "arbitrary"), 52),
        name="ple",
    )(h, gain, p, w_ple, w_gate, gain_final)


def rotary_tables(T):
    half = HEAD_DIM // 2
    inv_freq = ROPE_BASE ** (-jnp.arange(half, dtype=F32) / half)
    ang = jnp.arange(T, dtype=F32)[:, None] * inv_freq[None, :]
    cos, sin = jnp.cos(ang), jnp.sin(ang)
    return jnp.concatenate([cos, cos], axis=1), jnp.concatenate([-sin, sin], axis=1)


def kernel(x_prompt, x_sample, p_prompt, p_sample, norm_mix, w_in, ret_decay_logit, mlstm_conv_w, mlstm_conv_b, mlstm_gate_bias, s5_lambda_re, s5_lambda_im, s5_log_step, s5_b_re, s5_b_im, s5_c_re, s5_c_im, s5_d, s5_glu_w, s5_glu_b, na_rel_bias, mix_out_norm, w_out, norm_ffn, w_router, w_expert_gate, w_expert_up, w_expert_down, ple_norm, w_ple, w_ple_gate, norm_final):
    depth = w_in.shape[0]
    gate_lo = 8 * GROUP_W
    layers = []
    for l in range(depth):
        wl = w_in[l]
        w_main = jnp.concatenate([wl[:, :gate_lo], wl[:, gate_lo + N_GATE:]], axis=1).astype(BF16)
        wgp = jnp.pad(wl[:, gate_lo:gate_lo + N_GATE].astype(F32), ((0, 0), (0, LANES - N_GATE)))
        wg_hi = wgp.astype(BF16)
        wg_lo = (wgp - wg_hi.astype(F32)).astype(BF16)
        layers.append(dict(
            norm_mix=norm_mix[l].astype(F32).reshape(1, D_MODEL),
            w_main=w_main, wg_hi=wg_hi, wg_lo=wg_lo,
            s5=s5_tables(s5_lambda_re[l], s5_lambda_im[l], s5_log_step[l], s5_b_re[l], s5_b_im[l],
                         s5_c_re[l], s5_c_im[l]),
            na=na_bias_tables(na_rel_bias[l]),
            mix_gain=mix_out_norm[l].astype(F32).reshape(4, GROUP_W),
            w_out=w_out[l].astype(BF16),
            norm_ffn=norm_ffn[l].astype(F32).reshape(1, D_MODEL),
            w_router=jnp.pad(w_router[l].astype(F32), ((0, 0), (0, LANES - N_EXPERTS))),
            wg=w_expert_gate[l].astype(BF16), wu=w_expert_up[l].astype(BF16), wd=w_expert_down[l].astype(BF16),
            ple_norm=ple_norm[l].astype(F32).reshape(1, D_MODEL),
            w_ple=w_ple[l].astype(BF16), w_ple_gate=w_ple_gate[l].astype(BF16),
        ))
    g_final = norm_final.astype(F32).reshape(1, D_MODEL)

    def run(x, p):
        B, T, _ = x.shape
        assert B == 1
        h = x.reshape(T, D_MODEL)
        cos_t, sin_t = rotary_tables(T)
        for l, lw in enumerate(layers):
            proj, gates = in_proj(h, lw["norm_mix"], lw["w_main"], lw["wg_hi"], lw["wg_lo"])
            o_ret = retention(proj, ret_decay_logit[l], cos_t, sin_t)
            o_ml = mlstm(proj, gates, mlstm_gate_bias[l], mlstm_conv_w[l], mlstm_conv_b[l])
            y_s5 = s5_core(proj[:, CB_SU * LANES:CB_SU * LANES + GROUP_W], lw["s5"])
            o_s5 = s5_post(y_s5, proj, s5_d[l], s5_glu_w[l], s5_glu_b[l])
            o_na = neighbourhood_attention(proj, lw["na"])
            h = out_proj((o_ret, o_ml, o_s5, o_na), lw["mix_gain"], lw["w_out"], h)
            h = expert_choice_ffn(h, lw["norm_ffn"], lw["w_router"], lw["wg"], lw["wu"], lw["wd"])
            h = ple(h, lw["ple_norm"], p[l].reshape(T, PLE_DIM), lw["w_ple"], lw["w_ple_gate"], g_final,
                    final=(l == depth - 1))
        return h.reshape(B, T, D_MODEL)

    return (run(x_prompt, p_prompt), run(x_sample, p_sample))
```

```python
import functools
import math

import numpy as np
import jax
import jax.numpy as jnp
from jax import lax
from jax.experimental import pallas as pl
from jax.experimental.pallas import tpu as pltpu

F32 = jnp.float32
BF16 = jnp.bfloat16

D_MODEL = 4096
GROUP_W = 1024
HEAD_DIM = 128
N_HEADS = GROUP_W // HEAD_DIM
S5_CH = 16
S5_GROUPS = GROUP_W // S5_CH
S5_STATE = 64
GRID_W = 64
NA_KH = 8
NA_KW = 16
N_EXPERTS = 16
EC_CAPACITY = 2
D_EXPERT = D_MODEL // 2
PLE_DIM = 256
ROPE_BASE = 10000.0
EPS = 1e-6
M_INIT = -1e30
NEG = -1e30
LAMBDA_RE_MAX = -1e-4
N_MAIN = 12 * GROUP_W
N_GATE = 4 * N_HEADS
LANES = 128
MIB = 1024 * 1024

CB_RQ, CB_RK, CB_RV, CB_RG = 0, 8, 16, 24
CB_MQ, CB_MK, CB_MV, CB_MO = 32, 40, 48, 56
CB_SU = 64
CB_NQ, CB_NK, CB_NV = 72, 80, 88

S5_LC = 64
RET_L = 256
ML_L = 256
NA_QR = 8
NA_KR = 4
MOE_TM = 512
COMB_TB = 512
COMB_W = 128
COMB_EG = 2
ROW_CHUNK = 128


def _cparams(sem, vmem_mib):
    return pltpu.CompilerParams(dimension_semantics=sem, vmem_limit_bytes=vmem_mib * MIB)


def _dot(a, b):
    return jnp.dot(a, b, preferred_element_type=F32)


def _dot_nt(a, b):
    return lax.dot_general(a, b, (((1,), (1,)), ((), ())), preferred_element_type=F32)


def _dot_tn(a, b):
    return lax.dot_general(a, b, (((0,), (0,)), ((), ())), preferred_element_type=F32)


def _rms(x, g):
    ms = jnp.mean(x * x, axis=-1, keepdims=True)
    return x * lax.rsqrt(ms + EPS) * g


def _log_sigmoid(x):
    return jnp.minimum(x, 0.0) - jnp.log1p(jnp.exp(-jnp.abs(x)))


def _sigmoid(x):
    return 1.0 / (1.0 + jnp.exp(-x))


def _in_proj_kernel(x_ref, g_ref, w_ref, wgh_ref, wgl_ref, out_ref, gates_ref, xh_ref):
    j = pl.program_id(1)

    @pl.when(j == 0)
    def _():
        for r in range(0, x_ref.shape[0], ROW_CHUNK):
            rows = slice(r, r + ROW_CHUNK)
            xn = _rms(x_ref[rows, :], g_ref[...])
            xh = xn.astype(BF16)
            xh_ref[rows, :] = xh
            xl = (xn - xh.astype(F32)).astype(BF16)
            gates_ref[rows, :] = _dot(xh, wgh_ref[...]) + _dot(xl, wgh_ref[...]) + _dot(xh, wgl_ref[...])

    out_ref[...] = _dot(xh_ref[...], w_ref[...])


def in_proj(h, gain, w_main, wg_hi, wg_lo, tm=512, tn=1024):
    T, D = h.shape
    N = w_main.shape[1]
    return pl.pallas_call(
        _in_proj_kernel,
        grid=(T // tm, N // tn),
        in_specs=[
            pl.BlockSpec((tm, D), lambda i, j: (i, 0)),
            pl.BlockSpec((1, D), lambda i, j: (0, 0)),
            pl.BlockSpec((D, tn), lambda i, j: (0, j)),
            pl.BlockSpec((D, LANES), lambda i, j: (0, 0)),
            pl.BlockSpec((D, LANES), lambda i, j: (0, 0)),
        ],
        out_specs=[
            pl.BlockSpec((tm, tn), lambda i, j: (i, j)),
            pl.BlockSpec((tm, LANES), lambda i, j: (i, 0)),
        ],
        out_shape=[jax.ShapeDtypeStruct((T, N), F32), jax.ShapeDtypeStruct((T, LANES), F32)],
        scratch_shapes=[pltpu.VMEM((tm, D), BF16)],
        compiler_params=_cparams(("arbitrary", "arbitrary"), 52),
        name="in_proj",
    )(h, gain, w_main, wg_hi, wg_lo)


def _ret_kernel(dl_ref, q_ref, k_ref, v_ref, g_ref, cos_ref, sin_ref, out_ref,
                cb_all, cf, cb, *, L, nc):
    p = pl.program_id(0)
    jj = pl.program_id(1)
    lg = _log_sigmoid(dl_ref[...])
    lcol = lax.broadcasted_iota(jnp.int32, (L, 1), 0).astype(F32)
    cos = cos_ref[...]
    sin = sin_ref[...]

    def rot(x):
        return x * cos + pltpu.roll(x, HEAD_DIM // 2, 1) * sin

    @pl.when(jj == 0)
    def _():
        cb[...] = jnp.zeros_like(cb)
        cf[...] = jnp.zeros_like(cf)

    @pl.when(p == 0)
    def _():
        c = nc - 1 - jj
        for h in range(N_HEADS):
            hs = slice(h * HEAD_DIM, (h + 1) * HEAD_DIM)
            lgb = lg[1:2, h * HEAD_DIM:h * HEAD_DIM + 1]
            kr = rot(k_ref[:, hs]) * (HEAD_DIM ** -0.5)
            vb = v_ref[:, hs].astype(BF16)
            cb_all[c * N_HEADS + h] = cb[h].astype(BF16)
            kw = (kr * jnp.exp(lgb * lcol)).astype(BF16)
            cb[h] = jnp.exp(lgb * L) * cb[h] + _dot_tn(kw, vb)

    @pl.when(p == 1)
    def _():
        c = jj
        ii = lax.broadcasted_iota(jnp.int32, (L, L), 0)
        jx = lax.broadcasted_iota(jnp.int32, (L, L), 1)
        diff = (ii - jx).astype(F32)
        dpos = jnp.maximum(diff, 0.0)
        dneg = jnp.maximum(-diff, 0.0)
        for h in range(N_HEADS):
            hs = slice(h * HEAD_DIM, (h + 1) * HEAD_DIM)
            lgf = lg[0:1, h * HEAD_DIM:h * HEAD_DIM + 1]
            lgb = lg[1:2, h * HEAD_DIM:h * HEAD_DIM + 1]
            kr = rot(k_ref[:, hs]) * (HEAD_DIM ** -0.5)
            vb = v_ref[:, hs].astype(BF16)
            qr = rot(q_ref[:, hs])
            dmat = (jnp.where(diff >= 0, jnp.exp(lgf * dpos), 0.0)
                    + jnp.where(diff <= 0, jnp.exp(lgb * dneg), 0.0))
            s = _dot_nt(qr.astype(BF16), kr.astype(BF16)) * dmat
            o = _dot(s.astype(BF16), vb)
            o += _dot((qr * jnp.exp(lgf * (lcol + 1.0))).astype(BF16), cf[h].astype(BF16))
            o += _dot((qr * jnp.exp(lgb * (L - lcol))).astype(BF16), cb_all[c * N_HEADS + h])
            kw = (kr * jnp.exp(lgf * (L - 1.0 - lcol))).astype(BF16)
            cf[h] = jnp.exp(lgf * L) * cf[h] + _dot_tn(kw, vb)
            mu = jnp.mean(o, axis=-1, keepdims=True)
            oc = o - mu
            var = jnp.mean(oc * oc, axis=-1, keepdims=True)
            on = oc * lax.rsqrt(var + EPS)
            g = g_ref[:, hs]
            out_ref[:, hs] = g * _sigmoid(g) * on


def retention(proj, decay_logit, cos_t, sin_t, L=RET_L):
    T = proj.shape[0]
    nc = T // L
    dl = jnp.repeat(decay_logit.astype(F32), HEAD_DIM, axis=1)

    def chunk(p, jj):
        return p * jj + (1 - p) * (nc - 1 - jj)

    blk = (L, GROUP_W)
    return pl.pallas_call(
        functools.partial(_ret_kernel, L=L, nc=nc),
        grid=(2, nc),
        in_specs=[
            pl.BlockSpec((2, GROUP_W), lambda p, jj: (0, 0)),
            pl.BlockSpec(blk, lambda p, jj: (p * jj, CB_RQ // N_HEADS)),
            pl.BlockSpec(blk, lambda p, jj: (chunk(p, jj), CB_RK // N_HEADS)),
            pl.BlockSpec(blk, lambda p, jj: (chunk(p, jj), CB_RV // N_HEADS)),
            pl.BlockSpec(blk, lambda p, jj: (p * jj, CB_RG // N_HEADS)),
            pl.BlockSpec((L, HEAD_DIM), lambda p, jj: (chunk(p, jj), 0)),
            pl.BlockSpec((L, HEAD_DIM), lambda p, jj: (chunk(p, jj), 0)),
        ],
        out_specs=pl.BlockSpec(blk, lambda p, jj: (p * jj, 0)),
        out_shape=jax.ShapeDtypeStruct((T, GROUP_W), F32),
        scratch_shapes=[
            pltpu.VMEM((nc * N_HEADS, HEAD_DIM, HEAD_DIM), BF16),
            pltpu.VMEM((N_HEADS, HEAD_DIM, HEAD_DIM), F32),
            pltpu.VMEM((N_HEADS, HEAD_DIM, HEAD_DIM), F32),
        ],
        compiler_params=_cparams(("arbitrary", "arbitrary"), 48),
        name="retention",
    )(dl, proj, proj, proj, proj, cos_t, sin_t)


def _ml_conv(x, prev_row, next_row, w_ref, b_ref, L):
    ri = lax.broadcasted_iota(jnp.int32, (L, HEAD_DIM), 0)
    xm = jnp.where(ri == 0, prev_row, pltpu.roll(x, 1, 0))
    xp = jnp.where(ri == L - 1, next_row, pltpu.roll(x, L - 1, 0))
    y = b_ref[...] + xm * w_ref[0:1, :] + x * w_ref[1:2, :] + xp * w_ref[2:3, :]
    return y * _sigmoid(y)


def _ml_kernel(gt_ref, gb_ref, q_ref, qp_ref, qn_ref, k_ref, kp_ref, kn_ref, v_ref, o_ref,
               cw_ref, cbias_ref, out_ref,
               cb_all, nb_all, mb_all, c_st, n_st, m_st, *, L, nc):
    H = N_HEADS
    p = pl.program_id(0)
    jj = pl.program_id(1)
    c = p * jj + (1 - p) * (nc - 1 - jj)
    has_prev = (c > 0).astype(F32)
    has_next = (c < nc - 1).astype(F32)
    hp = lax.Precision.HIGHEST

    pre = gt_ref[...] + gb_ref[...]
    lf = _log_sigmoid(pre)
    pre_t = pre.T
    lf_t = _log_sigmoid(pre_t)
    ii = lax.broadcasted_iota(jnp.int32, (L, L), 0)
    jx = lax.broadcasted_iota(jnp.int32, (L, L), 1)
    lower = jx <= ii
    upper = jx >= ii
    cum_c = jnp.dot(jnp.where(lower, 1.0, 0.0), lf, precision=hp, preferred_element_type=F32)
    cum_r = jnp.dot(lf_t, jnp.where(upper, 1.0, 0.0), precision=hp, preferred_element_type=F32)
    rev_c = cum_c[L - 1:L, :] - cum_c + lf
    rev_r = cum_r[:, L - 1:L] - cum_r + lf_t

    def conv_head(x_ref, xp_ref, xn_ref, h, base):
        hs = slice(h * HEAD_DIM, (h + 1) * HEAD_DIM)
        ws = slice(base + h * HEAD_DIM, base + (h + 1) * HEAD_DIM)
        return _ml_conv(x_ref[:, hs], xp_ref[7:8, hs] * has_prev, xn_ref[0:1, hs] * has_next,
                        cw_ref[:, ws], cbias_ref[:, ws], L)

    def state_update(h, kc, vb, a_col, g_tot):
        hs = slice(h * HEAD_DIM, (h + 1) * HEAD_DIM)
        m_prev = m_st[0:1, h * HEAD_DIM:h * HEAD_DIM + 1]
        m_new = jnp.maximum(g_tot + m_prev, jnp.max(a_col, axis=0, keepdims=True))
        decay = jnp.exp(g_tot + m_prev - m_new)
        kwf = kc * jnp.exp(a_col - m_new)
        c_st[h] = decay * c_st[h] + _dot_tn(kwf.astype(BF16), vb)
        n_st[:, hs] = decay * n_st[:, hs] + jnp.sum(kwf, axis=0, keepdims=True)
        m_st[:, hs] = jnp.broadcast_to(m_new, (1, HEAD_DIM))

    @pl.when(jj == 0)
    def _():
        c_st[...] = jnp.zeros_like(c_st)
        n_st[...] = jnp.zeros_like(n_st)
        m_st[...] = jnp.full(m_st.shape, M_INIT, F32)

    @pl.when(p == 0)
    def _():
        nb_all[pl.ds(c, 1), :] = n_st[...]
        mb_all[pl.ds(c, 1), :] = m_st[...]
        for h in range(H):
            hs = slice(h * HEAD_DIM, (h + 1) * HEAD_DIM)
            cb_all[c * H + h] = c_st[h].astype(BF16)
            kc = conv_head(k_ref, kp_ref, kn_ref, h, GROUP_W) * (HEAD_DIM ** -0.5)
            vb = v_ref[:, hs].astype(BF16)
            rb_col = rev_c[:, 3 * H + h:3 * H + h + 1]
            g_tot = rb_col[0:1, :]
            state_update(h, kc, vb, g_tot - rb_col + pre[:, 2 * H + h:2 * H + h + 1], g_tot)

    @pl.when(p == 1)
    def _():
        n_next = nb_all[pl.ds(c, 1), :]
        m_next = mb_all[pl.ds(c, 1), :]
        for h in range(H):
            hs = slice(h * HEAD_DIM, (h + 1) * HEAD_DIM)
            kc = conv_head(k_ref, kp_ref, kn_ref, h, GROUP_W) * (HEAD_DIM ** -0.5)
            kb = kc.astype(BF16)
            vb = v_ref[:, hs].astype(BF16)
            qc = conv_head(q_ref, qp_ref, qn_ref, h, 0)
            qb = qc.astype(BF16)
            qk = _dot_nt(qb, kb)

            def direction(d_log, inter_log, c_prev, n_prev):
                rowmax = jnp.max(d_log, axis=1, keepdims=True)
                m = jnp.maximum(inter_log, rowmax)
                w_inter = jnp.exp(inter_log - m)
                s = qk * jnp.exp(d_log - m)
                num = _dot(s.astype(BF16), vb) + w_inter * _dot(qb, c_prev)
                qn = jnp.sum(s, axis=1, keepdims=True) + w_inter * jnp.sum(qc * n_prev, axis=1, keepdims=True)
                den = jnp.maximum(jnp.abs(qn), jnp.exp(-m))
                return num * (1.0 / den)

            b_col = cum_c[:, H + h:H + h + 1]
            b_row = cum_r[H + h:H + h + 1, :]
            d_log_f = jnp.where(lower, b_col - b_row + pre_t[h:h + 1, :], NEG)
            m_prev = m_st[0:1, h * HEAD_DIM:h * HEAD_DIM + 1]
            h_f = direction(d_log_f, b_col + m_prev, c_st[h].astype(BF16), n_st[:, hs])

            rb_col = rev_c[:, 3 * H + h:3 * H + h + 1]
            rb_row = rev_r[3 * H + h:3 * H + h + 1, :]
            d_log_b = jnp.where(upper, rb_col - rb_row + pre_t[2 * H + h:2 * H + h + 1, :], NEG)
            h_b = direction(d_log_b, rb_col + m_next[:, h * HEAD_DIM:h * HEAD_DIM + 1],
                            cb_all[c * H + h], n_next[:, hs])

            out_ref[:, hs] = _sigmoid(o_ref[:, hs]) * (h_f + h_b)

            g_tot = b_col[L - 1:L, :]
            state_update(h, kc, vb, g_tot - b_col + pre[:, h:h + 1], g_tot)


def mlstm(proj, gates, gate_bias, conv_w, conv_b, L=ML_L):
    T = proj.shape[0]
    nc = T // L
    hb = L // 8
    n8 = T // 8
    gb = jnp.pad(gate_bias.astype(F32).reshape(1, N_GATE), ((0, 0), (0, LANES - N_GATE)))
    cw = conv_w.astype(F32)
    cb = conv_b.astype(F32).reshape(1, 2 * GROUP_W)

    def chunk(p, jj):
        return p * jj + (1 - p) * (nc - 1 - jj)

    def prev8(p, jj):
        return jnp.maximum(chunk(p, jj) * hb - 1, 0)

    def next8(p, jj):
        return jnp.minimum((chunk(p, jj) + 1) * hb, n8 - 1)

    blk = (L, GROUP_W)
    halo = (8, GROUP_W)
    cq, ck, cv, co = (x // N_HEADS for x in (CB_MQ, CB_MK, CB_MV, CB_MO))
    return pl.pallas_call(
        functools.partial(_ml_kernel, L=L, nc=nc),
        grid=(2, nc),
        in_specs=[
            pl.BlockSpec((L, LANES), lambda p, jj: (chunk(p, jj), 0)),
            pl.BlockSpec((1, LANES), lambda p, jj: (0, 0)),
            pl.BlockSpec(blk, lambda p, jj: (p * jj, cq)),
            pl.BlockSpec(halo, lambda p, jj: (p * prev8(p, jj), cq)),
            pl.BlockSpec(halo, lambda p, jj: (p * next8(p, jj), cq)),
            pl.BlockSpec(blk, lambda p, jj: (chunk(p, jj), ck)),
            pl.BlockSpec(halo, lambda p, jj: (prev8(p, jj), ck)),
            pl.BlockSpec(halo, lambda p, jj: (next8(p, jj), ck)),
            pl.BlockSpec(blk, lambda p, jj: (chunk(p, jj), cv)),
            pl.BlockSpec(blk, lambda p, jj: (p * jj, co)),
            pl.BlockSpec((3, 2 * GROUP_W), lambda p, jj: (0, 0)),
            pl.BlockSpec((1, 2 * GROUP_W), lambda p, jj: (0, 0)),
        ],
        out_specs=pl.BlockSpec(blk, lambda p, jj: (p * jj, 0)),
        out_shape=jax.ShapeDtypeStruct((T, GROUP_W), F32),
        scratch_shapes=[
            pltpu.VMEM((nc * N_HEADS, HEAD_DIM, HEAD_DIM), BF16),
            pltpu.VMEM((nc, GROUP_W), F32),
            pltpu.VMEM((nc, GROUP_W), F32),
            pltpu.VMEM((N_HEADS, HEAD_DIM, HEAD_DIM), F32),
            pltpu.VMEM((1, GROUP_W), F32),
            pltpu.VMEM((1, GROUP_W), F32),
        ],
        compiler_params=_cparams(("arbitrary", "arbitrary"), 48),
        name="mlstm",
    )(gates, gb, proj, proj, proj, proj, proj, proj, proj, proj, cw, cb)


def _s5_unfold_kernel(seq_ref, mt_ref, *, Lc):
    seq = seq_ref[...]
    n = Lc * S5_CH
    for s in range(Lc):
        off = (Lc - s) * S5_CH
        mt_ref[s * S5_CH:(s + 1) * S5_CH, :] = seq[:, off:off + n].astype(BF16)


def s5_unfold(lagseq, Lc):
    G, C, W = lagseq.shape
    K = Lc * C
    return pl.pallas_call(
        functools.partial(_s5_unfold_kernel, Lc=Lc),
        grid=(G,),
        in_specs=[pl.BlockSpec((None, C, W), lambda g: (g, 0, 0))],
        out_specs=pl.BlockSpec((None, K, K), lambda g: (g, 0, 0)),
        out_shape=jax.ShapeDtypeStruct((G, K, K), BF16),
        compiler_params=_cparams(("arbitrary",), 32),
        name="s5_unfold",
    )(lagseq)


def s5_tables(lam_re, lam_im, log_step, b_re, b_im, c_re, c_im, Lc=S5_LC):
    f = lambda t: t.astype(F32)
    lam_re, lam_im, log_step, b_re, b_im, c_re, c_im = map(f, (lam_re, lam_im, log_step, b_re, b_im, c_re, c_im))
    G, P, C = S5_GROUPS, S5_STATE, S5_CH
    step = jnp.exp(log_step)[..., None]
    lre = jnp.minimum(lam_re, LAMBDA_RE_MAX)
    mag = jnp.exp(lre * step)
    ang = lam_im * step
    abar_re, abar_im = mag * jnp.cos(ang), mag * jnp.sin(ang)
    den = lre * lre + lam_im * lam_im
    zr, zi = abar_re - 1.0, abar_im
    coef_re = (zr * lre + zi * lam_im) / den
    coef_im = (zi * lre - zr * lam_im) / den
    bbar_re = coef_re[..., None] * b_re - coef_im[..., None] * b_im
    bbar_im = coef_re[..., None] * b_im + coef_im[..., None] * b_re
    n = jnp.arange(Lc + 1, dtype=F32)
    pmag = jnp.exp((lre * step)[..., None] * n)
    pang = ang[..., None] * n
    pw_re, pw_im = pmag * jnp.cos(pang), pmag * jnp.sin(pang)
    ct_re = jnp.swapaxes(c_re, -1, -2)[..., None]
    ct_im = jnp.swapaxes(c_im, -1, -2)[..., None]
    br_, bi_ = bbar_re[..., None, :], bbar_im[..., None, :]
    cb_re = ct_re * br_ - ct_im * bi_
    cb_im = ct_re * bi_ + ct_im * br_
    hp = lax.Precision.HIGHEST
    kk = (jnp.einsum('xgpn,xgpce->xgnce', pw_re[..., :Lc], cb_re, precision=hp)
          - jnp.einsum('xgpn,xgpce->xgnce', pw_im[..., :Lc], cb_im, precision=hp))
    fwd = jnp.pad(kk[0], ((0, 0), (Lc, 0), (0, 0), (0, 0)))
    bwd = jnp.pad(kk[1][:, ::-1], ((0, 0), (1, Lc - 1), (0, 0), (0, 0)))
    lagseq = (fwd + bwd).transpose(0, 3, 1, 2).reshape(G, C, 2 * Lc * C)
    mt = s5_unfold(lagseq, Lc)
    def summ(pr, pi, br, bi):
        pr, pi = (jnp.swapaxes(t, 1, 2)[:, :, None, :] for t in (pr, pi))
        br, bi = (jnp.swapaxes(t, 1, 2)[:, None, :, :] for t in (br, bi))
        return pr * br - pi * bi, pr * bi + pi * br
    f_re, f_im = summ(pw_re[0][..., Lc - 1::-1][..., :Lc], pw_im[0][..., Lc - 1::-1][..., :Lc], bbar_re[0], bbar_im[0])
    r_re, r_im = summ(pw_re[1][..., :Lc], pw_im[1][..., :Lc], bbar_re[1], bbar_im[1])
    wb = jnp.concatenate([f_re, f_im, r_re, r_im], axis=-1).reshape(G, Lc * C, 4 * P).astype(BF16)
    def read(pr, pi, cr, ci):
        pr, pi = pr[:, :, :, None], pi[:, :, :, None]
        cr, ci = (jnp.swapaxes(t, 1, 2)[:, :, None, :] for t in (cr, ci))
        return cr * pr - ci * pi, -(cr * pi + ci * pr)
    o_re, o_im = read(pw_re[0][..., 1:], pw_im[0][..., 1:], c_re[0], c_im[0])
    q_re, q_im = read(pw_re[1][..., Lc:0:-1], pw_im[1][..., Lc:0:-1], c_re[1], c_im[1])
    wc = jnp.concatenate([o_re, o_im, q_re, q_im], axis=1).reshape(G, 4 * P, Lc * C).astype(BF16)
    al_re, al_im = pw_re[..., Lc], pw_im[..., Lc]
    a1 = jnp.concatenate([al_re, al_re], axis=-1)
    a2 = jnp.concatenate([-al_im, al_im], axis=-1)
    return mt, wb, wc, a1, a2


def _s5_summary_kernel(x_ref, wb_ref, e_ref):
    e_ref[...] = _dot(x_ref[...], wb_ref[...])


def _s5_scan_kernel(ef_ref, eb_ref, a1_ref, a2_ref, sf_ref, sb_ref, *, nc):
    a1f, a2f = a1_ref[0], a2_ref[0]
    a1b, a2b = a1_ref[1], a2_ref[1]
    half = S5_STATE

    def body(j, carry):
        sf, sb = carry
        jb = nc - 1 - j
        sf_ref[j] = sf
        sb_ref[jb] = sb
        sf = a1f * sf + a2f * pltpu.roll(sf, half, 1) + ef_ref[j]
        sb = a1b * sb + a2b * pltpu.roll(sb, half, 1) + eb_ref[jb]
        return sf, sb

    z = jnp.zeros(a1f.shape, F32)
    lax.fori_loop(0, nc, body, (z, z))


def _s5_out_kernel(x_ref, mt_ref, s_ref, wc_ref, y_ref):
    y_ref[...] = _dot(x_ref[...], mt_ref[...]) + _dot(s_ref[...], wc_ref[...])


def s5_core(u_main, tables, Lc=S5_LC):
    mt, wb, wc, a1, a2 = tables
    T = u_main.shape[0]
    nc = T // Lc
    G, C, P = S5_GROUPS, S5_CH, S5_STATE
    K = Lc * C
    x = u_main.astype(BF16).reshape(nc, Lc, G, C).transpose(2, 0, 1, 3).reshape(G, nc, K)
    e = pl.pallas_call(
        _s5_summary_kernel,
        grid=(G,),
        in_specs=[pl.BlockSpec((None, nc, K), lambda g: (g, 0, 0)),
                  pl.BlockSpec((None, K, 4 * P), lambda g: (g, 0, 0))],
        out_specs=pl.BlockSpec((None, nc, 4 * P), lambda g: (g, 0, 0)),
        out_shape=jax.ShapeDtypeStruct((G, nc, 4 * P), F32),
        compiler_params=_cparams(("arbitrary",), 32),
        name="s5_summary",
    )(x, wb)
    et = e.transpose(1, 0, 2)
    ef, eb = et[..., :2 * P], et[..., 2 * P:]
    gs = 16
    sf, sb = pl.pallas_call(
        functools.partial(_s5_scan_kernel, nc=nc),
        grid=(G // gs,),
        in_specs=[pl.BlockSpec((nc, gs, 2 * P), lambda i: (0, i, 0)),
                  pl.BlockSpec((nc, gs, 2 * P), lambda i: (0, i, 0)),
                  pl.BlockSpec((2, gs, 2 * P), lambda i: (0, i, 0)),
                  pl.BlockSpec((2, gs, 2 * P), lambda i: (0, i, 0))],
        out_specs=[pl.BlockSpec((nc, gs, 2 * P), lambda i: (0, i, 0)),
                   pl.BlockSpec((nc, gs, 2 * P), lambda i: (0, i, 0))],
        out_shape=[jax.ShapeDtypeStruct((nc, G, 2 * P), F32)] * 2,
        compiler_params=_cparams(("arbitrary",), 32),
        name="s5_scan",
    )(ef, eb, a1, a2)
    s = jnp.concatenate([sf, sb], axis=-1).transpose(1, 0, 2).astype(BF16)
    y = pl.pallas_call(
        _s5_out_kernel,
        grid=(G,),
        in_specs=[pl.BlockSpec((None, nc, K), lambda g: (g, 0, 0)),
                  pl.BlockSpec((None, K, K), lambda g: (g, 0, 0)),
                  pl.BlockSpec((None, nc, 4 * P), lambda g: (g, 0, 0)),
                  pl.BlockSpec((None, 4 * P, K), lambda g: (g, 0, 0))],
        out_specs=pl.BlockSpec((None, nc, K), lambda g: (g, 0, 0)),
        out_shape=jax.ShapeDtypeStruct((G, nc, K), F32),
        compiler_params=_cparams(("arbitrary",), 32),
        name="s5_out",
    )(x, mt, s, wc)
    return y.reshape(G, nc, Lc, C).transpose(1, 2, 0, 3).reshape(T, GROUP_W)


def _s5_post_kernel(y_ref, u_ref, d_ref, w_ref, b_ref, out_ref):
    y = y_ref[...] + d_ref[...] * u_ref[...]
    y = jax.nn.gelu(y)
    z = _dot(y.astype(BF16), w_ref[...]) + b_ref[...]
    out_ref[...] = y * _sigmoid(z)


def s5_post(y, proj, d_skip, glu_w, glu_b, tm=512):
    T = y.shape[0]
    return pl.pallas_call(
        _s5_post_kernel,
        grid=(T // tm,),
        in_specs=[pl.BlockSpec((tm, GROUP_W), lambda i: (i, 0)),
                  pl.BlockSpec((tm, GROUP_W), lambda i: (i, CB_SU // 8)),
                  pl.BlockSpec((1, GROUP_W), lambda i: (0, 0)),
                  pl.BlockSpec((GROUP_W, GROUP_W), lambda i: (0, 0)),
                  pl.BlockSpec((1, GROUP_W), lambda i: (0, 0))],
        out_specs=pl.BlockSpec((tm, GROUP_W), lambda i: (i, 0)),
        out_shape=jax.ShapeDtypeStruct((T, GROUP_W), F32),
        compiler_params=_cparams(("arbitrary",), 32),
        name="s5_post",
    )(y, proj, d_skip.astype(F32).reshape(1, GROUP_W), glu_w.astype(BF16), glu_b.astype(F32).reshape(1, GROUP_W))


def na_bias_tables(rel_bias):
    rows = 4 * NA_QR
    kh, kw = NA_KH, NA_KW
    H = rel_bias.shape[0]
    n_dr = 2 * NA_KH - 1
    bp = jnp.pad(rel_bias.astype(F32), ((0, 0), (0, 0), (GRID_W - NA_KW, GRID_W - NA_KW)))
    toe = jnp.stack([bp[:, :, GRID_W - 1 - qc:2 * GRID_W - 1 - qc] for qc in range(GRID_W)], axis=2)
    qc = np.arange(GRID_W)[:, None]
    kc = np.arange(GRID_W)[None, :]
    cs = np.clip(qc - kw // 2, 0, GRID_W - kw)
    col_ok = (kc >= cs) & (kc < cs + kw)
    tiles = jnp.where(col_ok[None, None], toe, NEG)
    neg_tile = jnp.full((H, GRID_W, GRID_W), NEG, F32)
    out = []
    for r0 in (0, NA_QR, rows - NA_QR):
        row_blocks = []
        for i in range(NA_QR):
            qr = r0 + i
            rs = min(max(qr - kh // 2, 0), rows - kh)
            parts = []
            for j in range(4 * NA_KR):
                kr = r0 - NA_KR + j
                ok = 0 <= kr < rows and rs <= kr < rs + kh
                parts.append(tiles[:, kr - qr + NA_KH - 1] if ok else neg_tile)
            row_blocks.append(jnp.concatenate(parts, axis=2))
        out.append(jnp.concatenate(row_blocks, axis=1))
    return jnp.stack(out, axis=1)


def _na_kernel(q_ref, k0, k1, k2, k3, v0, v1, v2, v3, tab_ref, out_ref, *, kb):
    q = (q_ref[...] * (HEAD_DIM ** -0.5)).astype(BF16)
    ks = (k0, k1, k2, k3)
    vs = (v0, v1, v2, v3)
    s = jnp.concatenate([_dot_nt(q, kr[...].astype(BF16)) for kr in ks], axis=1) + tab_ref[...]
    m = jnp.max(s, axis=1, keepdims=True)
    p = jnp.exp(s - m)
    l = jnp.sum(p, axis=1, keepdims=True)
    pb = p.astype(BF16)
    o = _dot(pb[:, 0:kb], vs[0][...].astype(BF16))
    for i in range(1, 4):
        o += _dot(pb[:, i * kb:(i + 1) * kb], vs[i][...].astype(BF16))
    out_ref[...] = o * (1.0 / l)


def neighbourhood_attention(proj, tables):
    T = proj.shape[0]
    rows = T // GRID_W
    assert rows >= 3 * NA_QR and rows % NA_QR == 0
    tq = NA_QR * GRID_W
    kb = NA_KR * GRID_W
    nb = T // tq
    nkb = T // kb

    def kidx(b, i):
        return jnp.clip(2 * b - 1 + i, 0, nkb - 1)

    def pat(b):
        return jnp.where(b == 0, 0, jnp.where(b == nb - 1, 2, 1))

    kspecs = [pl.BlockSpec((kb, HEAD_DIM), functools.partial(lambda h, b, i: (kidx(b, i), CB_NK + h), i=i))
              for i in range(4)]
    vspecs = [pl.BlockSpec((kb, HEAD_DIM), functools.partial(lambda h, b, i: (kidx(b, i), CB_NV + h), i=i))
              for i in range(4)]
    return pl.pallas_call(
        functools.partial(_na_kernel, kb=kb),
        grid=(N_HEADS, nb),
        in_specs=[pl.BlockSpec((tq, HEAD_DIM), lambda h, b: (b, CB_NQ + h))] + kspecs + vspecs + [
            pl.BlockSpec((None, None, tq, 4 * kb), lambda h, b: (h, pat(b), 0, 0))],
        out_specs=pl.BlockSpec((tq, HEAD_DIM), lambda h, b: (b, h)),
        out_shape=jax.ShapeDtypeStruct((T, GROUP_W), F32),
        compiler_params=_cparams(("arbitrary", "arbitrary"), 40),
        name="neighbourhood_attention",
    )(proj, *([proj] * 8), tables)


def _out_proj_kernel(a0, a1, a2, a3, g_ref, w_ref, h_ref, out_ref, m_ref):
    j = pl.program_id(1)

    @pl.when(j == 0)
    def _():
        for gi, a in enumerate((a0, a1, a2, a3)):
            for r in range(0, a.shape[0], ROW_CHUNK):
                rows = slice(r, r + ROW_CHUNK)
                m_ref[rows, gi * GROUP_W:(gi + 1) * GROUP_W] = _rms(a[rows, :], g_ref[gi:gi + 1, :]).astype(BF16)

    out_ref[...] = h_ref[...] + _dot(m_ref[...], w_ref[...])


def out_proj(groups, gain, w_out, h, tm=512, tn=1024):
    T, D = h.shape
    gspec = pl.BlockSpec((tm, GROUP_W), lambda i, j: (i, 0))
    return pl.pallas_call(
        _out_proj_kernel,
        grid=(T // tm, D // tn),
        in_specs=[gspec, gspec, gspec, gspec,
                  pl.BlockSpec((4, GROUP_W), lambda i, j: (0, 0)),
                  pl.BlockSpec((D, tn), lambda i, j: (0, j)),
                  pl.BlockSpec((tm, tn), lambda i, j: (i, j))],
        out_specs=pl.BlockSpec((tm, tn), lambda i, j: (i, j)),
        out_shape=jax.ShapeDtypeStruct((T, D), F32),
        scratch_shapes=[pltpu.VMEM((tm, D), BF16)],
        compiler_params=_cparams(("arbitrary", "arbitrary"), 52),
        name="out_proj",
    )(*groups, gain, w_out, h)


def _router_kernel(h_ref, g_ref, wrh_ref, wrl_ref, xn_ref, aff_ref):
    for r in range(0, h_ref.shape[0], ROW_CHUNK):
        rows = slice(r, r + ROW_CHUNK)
        xn = _rms(h_ref[rows, :], g_ref[...])
        xn_ref[rows, :] = xn
        xh = xn.astype(BF16)
        xl = (xn - xh.astype(F32)).astype(BF16)
        logits = _dot(xh, wrh_ref[...]) + _dot(xl, wrh_ref[...]) + _dot(xh, wrl_ref[...])
        lane = lax.broadcasted_iota(jnp.int32, logits.shape, 1)
        logits = jnp.where(lane < N_EXPERTS, logits, NEG)
        m = jnp.max(logits, axis=1, keepdims=True)
        e = jnp.exp(logits - m)
        aff_ref[rows, :] = e / jnp.sum(e, axis=1, keepdims=True)


def router(h, gain, w_router_pad, tm=512):
    T, D = h.shape
    wr_hi = w_router_pad.astype(BF16)
    wr_lo = (w_router_pad - wr_hi.astype(F32)).astype(BF16)
    return pl.pallas_call(
        _router_kernel,
        grid=(T // tm,),
        in_specs=[pl.BlockSpec((tm, D), lambda i: (i, 0)),
                  pl.BlockSpec((1, D), lambda i: (0, 0)),
                  pl.BlockSpec((D, LANES), lambda i: (0, 0)),
                  pl.BlockSpec((D, LANES), lambda i: (0, 0))],
        out_specs=[pl.BlockSpec((tm, D), lambda i: (i, 0)),
                   pl.BlockSpec((tm, LANES), lambda i: (i, 0))],
        out_shape=[jax.ShapeDtypeStruct((T, D), F32), jax.ShapeDtypeStruct((T, LANES), F32)],
        compiler_params=_cparams(("arbitrary",), 48),
        name="router",
    )(h, gain, wr_hi, wr_lo)


def _moe_up_kernel(idx_ref, x_hbm, wg_ref, wu_ref, hid_ref, xbuf, xb16, sem, *, tm, n_tiles):
    e = pl.program_id(0)
    m = pl.program_id(1)
    f = pl.program_id(2)
    n = e * pl.num_programs(1) + m
    slot = n % 2

    def row_copy(tok, r, s):
        return pltpu.make_async_copy(x_hbm.at[pl.ds(tok, 1), :], xbuf.at[s, pl.ds(r, 1), :], sem.at[s])

    def issue(tile, s):
        base = tile * tm

        def body(r, c):
            row_copy(idx_ref[base + r], r, s).start()
            return c

        lax.fori_loop(0, tm, body, 0, unroll=8)

    def wait(s):
        def body(r, c):
            row_copy(0, r, s).wait()
            return c

        lax.fori_loop(0, tm, body, 0, unroll=8)

    @pl.when(f == 0)
    def _():
        @pl.when(n == 0)
        def _():
            issue(0, 0)

        @pl.when(n + 1 < n_tiles)
        def _():
            issue(n + 1, 1 - slot)

        wait(slot)
        xb16[...] = xbuf[slot].astype(BF16)

    x = xb16[...]
    a = _dot(x, wg_ref[...])
    b = _dot(x, wu_ref[...])
    hid_ref[...] = (a * _sigmoid(a) * b).astype(BF16)


def moe_up(idx_flat, xn, wg, wu, cap, tm=MOE_TM, tf=512):
    E, D, F = wg.shape
    n_m = cap // tm
    grid_spec = pltpu.PrefetchScalarGridSpec(
        num_scalar_prefetch=1,
        grid=(E, n_m, F // tf),
        in_specs=[pl.BlockSpec(memory_space=pl.ANY),
                  pl.BlockSpec((None, D, tf), lambda e, m, f, idx: (e, 0, f)),
                  pl.BlockSpec((None, D, tf), lambda e, m, f, idx: (e, 0, f))],
        out_specs=pl.BlockSpec((None, tm, tf), lambda e, m, f, idx: (e, m, f)),
        scratch_shapes=[pltpu.VMEM((2, tm, D), F32), pltpu.VMEM((tm, D), BF16), pltpu.SemaphoreType.DMA((2,))],
    )
    return pl.pallas_call(
        functools.partial(_moe_up_kernel, tm=tm, n_tiles=E * n_m),
        grid_spec=grid_spec,
        out_shape=jax.ShapeDtypeStruct((E, cap, F), BF16),
        compiler_params=_cparams(("arbitrary", "arbitrary", "arbitrary"), 48),
        name="moe_up",
    )(idx_flat, xn, wg, wu)


def _moe_down_kernel(hid_ref, wd_ref, gate_ref, ye_ref):
    ye_ref[...] = (_dot(hid_ref[...], wd_ref[...]) * gate_ref[...]).astype(BF16)


def moe_down(hid, wd, gate, tm=1024, tn=1024):
    E, cap, F = hid.shape
    D = wd.shape[2]
    tm = min(tm, cap)
    return pl.pallas_call(
        _moe_down_kernel,
        grid=(E, cap // tm, D // tn),
        in_specs=[pl.BlockSpec((None, tm, F), lambda e, m, n: (e, m, 0)),
                  pl.BlockSpec((None, F, tn), lambda e, m, n: (e, 0, n)),
                  pl.BlockSpec((None, tm, 1), lambda e, m, n: (e, m, 0))],
        out_specs=pl.BlockSpec((None, tm, tn), lambda e, m, n: (e, m, n)),
        out_shape=jax.ShapeDtypeStruct((E, cap, D), BF16),
        compiler_params=_cparams(("arbitrary", "arbitrary", "arbitrary"), 40),
        name="moe_down",
    )(hid, wd, gate)


def _combine_kernel(st_ref, h_ref, *refs, tb, W, nbp1, n_win, cn, eg):
    idx_refs = refs[:2 * eg]
    y_refs = refs[2 * eg:4 * eg]
    idx_hbm, ye_hbm, out_ref, ybuf, ibuf, sem = refs[4 * eg:]
    b = pl.program_id(0)
    ep = pl.program_id(1)
    D = out_ref.shape[1]

    @pl.when(ep == 0)
    def _():
        out_ref[...] = h_ref[...]

    tok = b * tb + lax.broadcasted_iota(jnp.int32, (tb, 1), 0)

    def onehot(idx_row, valid):
        return jnp.where(tok == idx_row * valid + (valid - 1), 1.0, 0.0).astype(BF16)

    ohs = []
    for k in range(eg):
        wa = st_ref[(ep * eg + k) * nbp1 + b] // W
        ohs.append(onehot(idx_refs[2 * k][...], (wa < n_win).astype(jnp.int32)))
        ohs.append(onehot(idx_refs[2 * k + 1][...], (wa + 1 < n_win).astype(jnp.int32)))
    oh = jnp.concatenate(ohs, axis=1)
    for c0 in range(0, D, cn):
        ycat = jnp.concatenate([y[:, c0:c0 + cn] for y in y_refs], axis=0)
        out_ref[:, c0:c0 + cn] += _dot(oh, ycat)

    for k in range(eg):
        e = ep * eg + k
        wa = st_ref[e * nbp1 + b] // W
        need = (st_ref[e * nbp1 + b + 1] + W - 1) // W - wa

        @pl.when(need > 2)
        def _(e=e, wa=wa, need=need):
            def body(w, c):
                off = pl.multiple_of((wa + w) * W, W)
                cy = pltpu.make_async_copy(ye_hbm.at[e, pl.ds(off, W), :], ybuf, sem.at[0])
                ci = pltpu.make_async_copy(idx_hbm.at[e, :, pl.ds(off, W)], ibuf, sem.at[1])
                cy.start()
                ci.start()
                cy.wait()
                ci.wait()
                ohw = onehot(ibuf[...], 1)
                for c0 in range(0, D, cn):
                    out_ref[:, c0:c0 + cn] += _dot(ohw, ybuf[:, c0:c0 + cn])
                return c

            lax.fori_loop(2, need, body, 0)


def moe_combine(starts_flat, h, idx_s, ye, tb=COMB_TB, W=COMB_W):
    T, D = h.shape
    E, cap, _ = ye.shape
    nb = T // tb
    n_win = cap // W
    idx4 = idx_s.reshape(E, n_win, 1, W)
    idx3 = idx_s.reshape(E, 1, cap)

    eg = COMB_EG

    def win(b, ep, st, k, j):
        return jnp.minimum(st[(ep * eg + k) * (nb + 1) + b] // W + j, n_win - 1)

    idx_specs, y_specs = [], []
    for k in range(eg):
        for j in range(2):
            idx_specs.append(pl.BlockSpec(
                (None, None, 1, W), functools.partial(lambda b, ep, st, k, j: (ep * eg + k, win(b, ep, st, k, j), 0, 0), k=k, j=j)))
            y_specs.append(pl.BlockSpec(
                (None, W, D), functools.partial(lambda b, ep, st, k, j: (ep * eg + k, win(b, ep, st, k, j), 0), k=k, j=j)))
    grid_spec = pltpu.PrefetchScalarGridSpec(
        num_scalar_prefetch=1,
        grid=(nb, E // eg),
        in_specs=[pl.BlockSpec((tb, D), lambda b, ep, st: (b, 0))] + idx_specs + y_specs + [
            pl.BlockSpec(memory_space=pl.ANY), pl.BlockSpec(memory_space=pl.ANY)],
        out_specs=pl.BlockSpec((tb, D), lambda b, ep, st: (b, 0)),
        scratch_shapes=[pltpu.VMEM((W, D), BF16), pltpu.VMEM((1, W), jnp.int32), pltpu.SemaphoreType.DMA((2,))],
    )
    return pl.pallas_call(
        functools.partial(_combine_kernel, tb=tb, W=W, nbp1=nb + 1, n_win=n_win, cn=1024, eg=eg),
        grid_spec=grid_spec,
        out_shape=jax.ShapeDtypeStruct((T, D), F32),
        compiler_params=_cparams(("arbitrary", "arbitrary"), 52),
        name="moe_combine",
    )(starts_flat, h, *([idx4] * (2 * eg)), *([ye] * (2 * eg)), idx3, ye)


def expert_choice_ffn(h, gain, w_router_pad, wg, wu, wd):
    T, D = h.shape
    E = N_EXPERTS
    cap = EC_CAPACITY * T // E
    xn, aff = router(h, gain, w_router_pad)
    gate, idx = lax.top_k(aff[:, :E].T, cap)
    order = jnp.argsort(idx, axis=-1)
    idx_s = jnp.take_along_axis(idx, order, axis=-1).astype(jnp.int32)
    gate_s = jnp.take_along_axis(gate, order, axis=-1)
    hid = moe_up(idx_s.reshape(-1), xn, wg, wu, cap)
    ye = moe_down(hid, wd, gate_s[..., None])
    nb = T // COMB_TB
    edges = (jnp.arange(nb + 1, dtype=jnp.int32) * COMB_TB)
    starts = jnp.sum(idx_s[:, :, None] < edges[None, None, :], axis=1).astype(jnp.int32)
    return moe_combine(starts.reshape(-1), h, idx_s, ye)


def _ple_kernel(h_ref, g_ref, p_ref, wp_ref, wg_ref, gf_ref, out_ref, hn_ref, *, tn, final):
    j = pl.program_id(1)

    tm = h_ref.shape[0]

    @pl.when(j == 0)
    def _():
        for r in range(0, tm, ROW_CHUNK):
            rows = slice(r, r + ROW_CHUNK)
            hn_ref[rows, :] = _rms(h_ref[rows, :], g_ref[...]).astype(BF16)

    gate = _sigmoid(_dot(hn_ref[...], wg_ref[...]))
    emb = _dot(p_ref[...].astype(BF16), wp_ref[...])
    c0 = pl.multiple_of(j * tn, tn)
    out_ref[:, pl.ds(c0, tn)] = h_ref[:, pl.ds(c0, tn)] + emb * gate

    if final:
        @pl.when(j == pl.num_programs(1) - 1)
        def _():
            for r in range(0, tm, ROW_CHUNK):
                rows = slice(r, r + ROW_CHUNK)
                out_ref[rows, :] = _rms(out_ref[rows, :], gf_ref[...])


def ple(h, gain, p, w_ple, w_gate, gain_final, final, tm=512, tn=512):
    T, D = h.shape
    return pl.pallas_call(
        functools.partial(_ple_kernel, tn=tn, final=final),
        grid=(T // tm, D // tn),
        in_specs=[pl.BlockSpec((tm, D), lambda i, j: (i, 0)),
                  pl.BlockSpec((1, D), lambda i, j: (0, 0)),
                  pl.BlockSpec((tm, PLE_DIM), lambda i, j: (i, 0)),
                  pl.BlockSpec((PLE_DIM, tn), lambda i, j: (0, j)),
                  pl.BlockSpec((D, tn), lambda i, j: (0, j)),
                  pl.BlockSpec((1, D), lambda i, j: (0, 0))],
        out_specs=pl.BlockSpec((tm, D), lambda i, j: (i, 0)),
        out_shape=jax.ShapeDtypeStruct((T, D), F32),
        scratch_shapes=[pltpu.VMEM((tm, D), BF16)],
        compiler_params=_cparams(("arbitrary", "arbitrary"), 52),
        name="ple",
    )(h, gain, p, w_ple, w_gate, gain_final)


def rotary_tables(T):
    half = HEAD_DIM // 2
    inv_freq = ROPE_BASE ** (-jnp.arange(half, dtype=F32) / half)
    ang = jnp.arange(T, dtype=F32)[:, None] * inv_freq[None, :]
    cos, sin = jnp.cos(ang), jnp.sin(ang)
    return jnp.concatenate([cos, cos], axis=1), jnp.concatenate([-sin, sin], axis=1)


def kernel(x_prompt, x_sample, p_prompt, p_sample, norm_mix, w_in, ret_decay_logit, mlstm_conv_w, mlstm_conv_b, mlstm_gate_bias, s5_lambda_re, s5_lambda_im, s5_log_step, s5_b_re, s5_b_im, s5_c_re, s5_c_im, s5_d, s5_glu_w, s5_glu_b, na_rel_bias, mix_out_norm, w_out, norm_ffn, w_router, w_expert_gate, w_expert_up, w_expert_down, ple_norm, w_ple, w_ple_gate, norm_final):
    depth = w_in.shape[0]
    gate_lo = 8 * GROUP_W
    layers = []
    for l in range(depth):
        wl = w_in[l]
        w_main = jnp.concatenate([wl[:, :gate_lo], wl[:, gate_lo + N_GATE:]], axis=1).astype(BF16)
        wgp = jnp.pad(wl[:, gate_lo:gate_lo + N_GATE].astype(F32), ((0, 0), (0, LANES - N_GATE)))
        wg_hi = wgp.astype(BF16)
        wg_lo = (wgp - wg_hi.astype(F32)).astype(BF16)
        layers.append(dict(
            norm_mix=norm_mix[l].astype(F32).reshape(1, D_MODEL),
            w_main=w_main, wg_hi=wg_hi, wg_lo=wg_lo,
            s5=s5_tables(s5_lambda_re[l], s5_lambda_im[l], s5_log_step[l], s5_b_re[l], s5_b_im[l],
                         s5_c_re[l], s5_c_im[l]),
            na=na_bias_tables(na_rel_bias[l]),
            mix_gain=mix_out_norm[l].astype(F32).reshape(4, GROUP_W),
            w_out=w_out[l].astype(BF16),
            norm_ffn=norm_ffn[l].astype(F32).reshape(1, D_MODEL),
            w_router=jnp.pad(w_router[l].astype(F32), ((0, 0), (0, LANES - N_EXPERTS))),
            wg=w_expert_gate[l].astype(BF16), wu=w_expert_up[l].astype(BF16), wd=w_expert_down[l].astype(BF16),
            ple_norm=ple_norm[l].astype(F32).reshape(1, D_MODEL),
            w_ple=w_ple[l].astype(BF16), w_ple_gate=w_ple_gate[l].astype(BF16),
        ))
    g_final = norm_final.astype(F32).reshape(1, D_MODEL)

    def run(x, p):
        B, T, _ = x.shape
        assert B == 1
        h = x.reshape(T, D_MODEL)
        cos_t, sin_t = rotary_tables(T)
        for l, lw in enumerate(layers):
            proj, gates = in_proj(h, lw["norm_mix"], lw["w_main"], lw["wg_hi"], lw["wg_lo"])
            o_ret = retention(proj, ret_decay_logit[l], cos_t, sin_t)
            o_ml = mlstm(proj, gates, mlstm_gate_bias[l], mlstm_conv_w[l], mlstm_conv_b[l])
            y_s5 = s5_core(proj[:, CB_SU * LANES:CB_SU * LANES + GROUP_W], lw["s5"])
            o_s5 = s5_post(y_s5, proj, s5_d[l], s5_glu_w[l], s5_glu_b[l])
            o_na = neighbourhood_attention(proj, lw["na"])
            h = out_proj((o_ret, o_ml, o_s5, o_na), lw["mix_gain"], lw["w_out"], h)
            h = expert_choice_ffn(h, lw["norm_ffn"], lw["w_router"], lw["wg"], lw["wu"], lw["wd"])
            h = ple(h, lw["ple_norm"], p[l].reshape(T, PLE_DIM), lw["w_ple"], lw["w_ple_gate"], g_final,
                    final=(l == depth - 1))
        return h.reshape(B, T, D_MODEL)

    return (run(x_prompt, p_prompt), run(x_sample, p_sample))
```

```python
import functools
import math

import numpy as np
import jax
import jax.numpy as jnp
from jax import lax
from jax.experimental import pallas as pl
from jax.experimental.pallas import tpu as pltpu

F32 = jnp.float32
BF16 = jnp.bfloat16

D_MODEL = 4096
GROUP_W = 1024
HEAD_DIM = 128
N_HEADS = GROUP_W // HEAD_DIM
S5_CH = 16
S5_GROUPS = GROUP_W // S5_CH
S5_STATE = 64
GRID_W = 64
NA_KH = 8
NA_KW = 16
N_EXPERTS = 16
EC_CAPACITY = 2
D_EXPERT = D_MODEL // 2
PLE_DIM = 256
ROPE_BASE = 10000.0
EPS = 1e-6
M_INIT = -1e30
NEG = -1e30
LAMBDA_RE_MAX = -1e-4
N_MAIN = 12 * GROUP_W
N_GATE = 4 * N_HEADS
LANES = 128
MIB = 1024 * 1024

CB_RQ, CB_RK, CB_RV, CB_RG = 0, 8, 16, 24
CB_MQ, CB_MK, CB_MV, CB_MO = 32, 40, 48, 56
CB_SU = 64
CB_NQ, CB_NK, CB_NV = 72, 80, 88

S5_LC = 64
RET_L = 256
ML_L = 256
NA_QR = 8
NA_KR = 4
MOE_TM = 512
COMB_TB = 256
COMB_NH = 1
COMB_W = 64
COMB_EG = 8
ROW_CHUNK = 128


def _cparams(sem, vmem_mib):
    return pltpu.CompilerParams(dimension_semantics=sem, vmem_limit_bytes=vmem_mib * MIB)


def _dot(a, b):
    return jnp.dot(a, b, preferred_element_type=F32)


def _dot_nt(a, b):
    return lax.dot_general(a, b, (((1,), (1,)), ((), ())), preferred_element_type=F32)


def _dot_tn(a, b):
    return lax.dot_general(a, b, (((0,), (0,)), ((), ())), preferred_element_type=F32)


def _rms(x, g):
    ms = jnp.mean(x * x, axis=-1, keepdims=True)
    return x * lax.rsqrt(ms + EPS) * g


def _log_sigmoid(x):
    return jnp.minimum(x, 0.0) - jnp.log1p(jnp.exp(-jnp.abs(x)))


def _sigmoid(x):
    return 1.0 / (1.0 + jnp.exp(-x))


def _in_proj_kernel(x_ref, g_ref, w_ref, wgh_ref, wgl_ref, out_ref, gates_ref, xh_ref):
    j = pl.program_id(1)

    @pl.when(j == 0)
    def _():
        for r in range(0, x_ref.shape[0], ROW_CHUNK):
            rows = slice(r, r + ROW_CHUNK)
            xn = _rms(x_ref[rows, :], g_ref[...])
            xh = xn.astype(BF16)
            xh_ref[rows, :] = xh
            xl = (xn - xh.astype(F32)).astype(BF16)
            gates_ref[rows, :] = _dot(xh, wgh_ref[...]) + _dot(xl, wgh_ref[...]) + _dot(xh, wgl_ref[...])

    out_ref[...] = _dot(xh_ref[...], w_ref[...])


def in_proj(h, gain, w_main, wg_hi, wg_lo, tm=512, tn=1024):
    T, D = h.shape
    N = w_main.shape[1]
    return pl.pallas_call(
        _in_proj_kernel,
        grid=(T // tm, N // tn),
        in_specs=[
            pl.BlockSpec((tm, D), lambda i, j: (i, 0)),
            pl.BlockSpec((1, D), lambda i, j: (0, 0)),
            pl.BlockSpec((D, tn), lambda i, j: (0, j)),
            pl.BlockSpec((D, LANES), lambda i, j: (0, 0)),
            pl.BlockSpec((D, LANES), lambda i, j: (0, 0)),
        ],
        out_specs=[
            pl.BlockSpec((tm, tn), lambda i, j: (i, j)),
            pl.BlockSpec((tm, LANES), lambda i, j: (i, 0)),
        ],
        out_shape=[jax.ShapeDtypeStruct((T, N), F32), jax.ShapeDtypeStruct((T, LANES), F32)],
        scratch_shapes=[pltpu.VMEM((tm, D), BF16)],
        compiler_params=_cparams(("arbitrary", "arbitrary"), 52),
        name="in_proj",
    )(h, gain, w_main, wg_hi, wg_lo)


def _ret_kernel(dl_ref, q_ref, k_ref, v_ref, g_ref, cos_ref, sin_ref, out_ref,
                cb_all, cf, cb, *, L, nc):
    p = pl.program_id(0)
    jj = pl.program_id(1)
    lg = _log_sigmoid(dl_ref[...])
    lcol = lax.broadcasted_iota(jnp.int32, (L, 1), 0).astype(F32)
    cos = cos_ref[...]
    sin = sin_ref[...]

    def rot(x):
        return x * cos + pltpu.roll(x, HEAD_DIM // 2, 1) * sin

    @pl.when(jj == 0)
    def _():
        cb[...] = jnp.zeros_like(cb)
        cf[...] = jnp.zeros_like(cf)

    @pl.when(p == 0)
    def _():
        c = nc - 1 - jj
        for h in range(N_HEADS):
            hs = slice(h * HEAD_DIM, (h + 1) * HEAD_DIM)
            lgb = lg[1:2, h * HEAD_DIM:h * HEAD_DIM + 1]
            kr = rot(k_ref[:, hs]) * (HEAD_DIM ** -0.5)
            vb = v_ref[:, hs].astype(BF16)
            cb_all[c * N_HEADS + h] = cb[h].astype(BF16)
            kw = (kr * jnp.exp(lgb * lcol)).astype(BF16)
            cb[h] = jnp.exp(lgb * L) * cb[h] + _dot_tn(kw, vb)

    @pl.when(p == 1)
    def _():
        c = jj
        ii = lax.broadcasted_iota(jnp.int32, (L, L), 0)
        jx = lax.broadcasted_iota(jnp.int32, (L, L), 1)
        diff = (ii - jx).astype(F32)
        dpos = jnp.maximum(diff, 0.0)
        dneg = jnp.maximum(-diff, 0.0)
        for h in range(N_HEADS):
            hs = slice(h * HEAD_DIM, (h + 1) * HEAD_DIM)
            lgf = lg[0:1, h * HEAD_DIM:h * HEAD_DIM + 1]
            lgb = lg[1:2, h * HEAD_DIM:h * HEAD_DIM + 1]
            kr = rot(k_ref[:, hs]) * (HEAD_DIM ** -0.5)
            vb = v_ref[:, hs].astype(BF16)
            qr = rot(q_ref[:, hs])
            dmat = (jnp.where(diff >= 0, jnp.exp(lgf * dpos), 0.0)
                    + jnp.where(diff <= 0, jnp.exp(lgb * dneg), 0.0))
            s = _dot_nt(qr.astype(BF16), kr.astype(BF16)) * dmat
            o = _dot(s.astype(BF16), vb)
            o += _dot((qr * jnp.exp(lgf * (lcol + 1.0))).astype(BF16), cf[h].astype(BF16))
            o += _dot((qr * jnp.exp(lgb * (L - lcol))).astype(BF16), cb_all[c * N_HEADS + h])
            kw = (kr * jnp.exp(lgf * (L - 1.0 - lcol))).astype(BF16)
            cf[h] = jnp.exp(lgf * L) * cf[h] + _dot_tn(kw, vb)
            mu = jnp.mean(o, axis=-1, keepdims=True)
            oc = o - mu
            var = jnp.mean(oc * oc, axis=-1, keepdims=True)
            on = oc * lax.rsqrt(var + EPS)
            g = g_ref[:, hs]
            out_ref[:, hs] = g * _sigmoid(g) * on


def retention(proj, decay_logit, cos_t, sin_t, L=RET_L):
    T = proj.shape[0]
    nc = T // L
    dl = jnp.repeat(decay_logit.astype(F32), HEAD_DIM, axis=1)

    def chunk(p, jj):
        return p * jj + (1 - p) * (nc - 1 - jj)

    blk = (L, GROUP_W)
    return pl.pallas_call(
        functools.partial(_ret_kernel, L=L, nc=nc),
        grid=(2, nc),
        in_specs=[
            pl.BlockSpec((2, GROUP_W), lambda p, jj: (0, 0)),
            pl.BlockSpec(blk, lambda p, jj: (p * jj, CB_RQ // N_HEADS)),
            pl.BlockSpec(blk, lambda p, jj: (chunk(p, jj), CB_RK // N_HEADS)),
            pl.BlockSpec(blk, lambda p, jj: (chunk(p, jj), CB_RV // N_HEADS)),
            pl.BlockSpec(blk, lambda p, jj: (p * jj, CB_RG // N_HEADS)),
            pl.BlockSpec((L, HEAD_DIM), lambda p, jj: (chunk(p, jj), 0)),
            pl.BlockSpec((L, HEAD_DIM), lambda p, jj: (chunk(p, jj), 0)),
        ],
        out_specs=pl.BlockSpec(blk, lambda p, jj: (p * jj, 0)),
        out_shape=jax.ShapeDtypeStruct((T, GROUP_W), F32),
        scratch_shapes=[
            pltpu.VMEM((nc * N_HEADS, HEAD_DIM, HEAD_DIM), BF16),
            pltpu.VMEM((N_HEADS, HEAD_DIM, HEAD_DIM), F32),
            pltpu.VMEM((N_HEADS, HEAD_DIM, HEAD_DIM), F32),
        ],
        compiler_params=_cparams(("arbitrary", "arbitrary"), 48),
        name="retention",
    )(dl, proj, proj, proj, proj, cos_t, sin_t)


def _ml_conv(x, prev_row, next_row, w_ref, b_ref, L):
    ri = lax.broadcasted_iota(jnp.int32, (L, HEAD_DIM), 0)
    xm = jnp.where(ri == 0, prev_row, pltpu.roll(x, 1, 0))
    xp = jnp.where(ri == L - 1, next_row, pltpu.roll(x, L - 1, 0))
    y = b_ref[...] + xm * w_ref[0:1, :] + x * w_ref[1:2, :] + xp * w_ref[2:3, :]
    return y * _sigmoid(y)


def _ml_kernel(gt_ref, gb_ref, q_ref, qp_ref, qn_ref, k_ref, kp_ref, kn_ref, v_ref, o_ref,
               cw_ref, cbias_ref, out_ref,
               cb_all, nb_all, mb_all, c_st, n_st, m_st, *, L, nc):
    H = N_HEADS
    p = pl.program_id(0)
    jj = pl.program_id(1)
    c = p * jj + (1 - p) * (nc - 1 - jj)
    has_prev = (c > 0).astype(F32)
    has_next = (c < nc - 1).astype(F32)
    hp = lax.Precision.HIGHEST

    pre = gt_ref[...] + gb_ref[...]
    lf = _log_sigmoid(pre)
    pre_t = pre.T
    lf_t = _log_sigmoid(pre_t)
    ii = lax.broadcasted_iota(jnp.int32, (L, L), 0)
    jx = lax.broadcasted_iota(jnp.int32, (L, L), 1)
    lower = jx <= ii
    upper = jx >= ii
    cum_c = jnp.dot(jnp.where(lower, 1.0, 0.0), lf, precision=hp, preferred_element_type=F32)
    cum_r = jnp.dot(lf_t, jnp.where(upper, 1.0, 0.0), precision=hp, preferred_element_type=F32)
    rev_c = cum_c[L - 1:L, :] - cum_c + lf
    rev_r = cum_r[:, L - 1:L] - cum_r + lf_t

    def conv_head(x_ref, xp_ref, xn_ref, h, base):
        hs = slice(h * HEAD_DIM, (h + 1) * HEAD_DIM)
        ws = slice(base + h * HEAD_DIM, base + (h + 1) * HEAD_DIM)
        return _ml_conv(x_ref[:, hs], xp_ref[7:8, hs] * has_prev, xn_ref[0:1, hs] * has_next,
                        cw_ref[:, ws], cbias_ref[:, ws], L)

    def state_update(h, kc, vb, a_col, g_tot):
        hs = slice(h * HEAD_DIM, (h + 1) * HEAD_DIM)
        m_prev = m_st[0:1, h * HEAD_DIM:h * HEAD_DIM + 1]
        m_new = jnp.maximum(g_tot + m_prev, jnp.max(a_col, axis=0, keepdims=True))
        decay = jnp.exp(g_tot + m_prev - m_new)
        kwf = kc * jnp.exp(a_col - m_new)
        c_st[h] = decay * c_st[h] + _dot_tn(kwf.astype(BF16), vb)
        n_st[:, hs] = decay * n_st[:, hs] + jnp.sum(kwf, axis=0, keepdims=True)
        m_st[:, hs] = jnp.broadcast_to(m_new, (1, HEAD_DIM))

    @pl.when(jj == 0)
    def _():
        c_st[...] = jnp.zeros_like(c_st)
        n_st[...] = jnp.zeros_like(n_st)
        m_st[...] = jnp.full(m_st.shape, M_INIT, F32)

    @pl.when(p == 0)
    def _():
        nb_all[pl.ds(c, 1), :] = n_st[...]
        mb_all[pl.ds(c, 1), :] = m_st[...]
        for h in range(H):
            hs = slice(h * HEAD_DIM, (h + 1) * HEAD_DIM)
            cb_all[c * H + h] = c_st[h].astype(BF16)
            kc = conv_head(k_ref, kp_ref, kn_ref, h, GROUP_W) * (HEAD_DIM ** -0.5)
            vb = v_ref[:, hs].astype(BF16)
            rb_col = rev_c[:, 3 * H + h:3 * H + h + 1]
            g_tot = rb_col[0:1, :]
            state_update(h, kc, vb, g_tot - rb_col + pre[:, 2 * H + h:2 * H + h + 1], g_tot)

    @pl.when(p == 1)
    def _():
        n_next = nb_all[pl.ds(c, 1), :]
        m_next = mb_all[pl.ds(c, 1), :]
        for h in range(H):
            hs = slice(h * HEAD_DIM, (h + 1) * HEAD_DIM)
            kc = conv_head(k_ref, kp_ref, kn_ref, h, GROUP_W) * (HEAD_DIM ** -0.5)
            kb = kc.astype(BF16)
            vb = v_ref[:, hs].astype(BF16)
            qc = conv_head(q_ref, qp_ref, qn_ref, h, 0)
            qb = qc.astype(BF16)
            qk = _dot_nt(qb, kb)

            def direction(d_log, inter_log, c_prev, n_prev):
                rowmax = jnp.max(d_log, axis=1, keepdims=True)
                m = jnp.maximum(inter_log, rowmax)
                w_inter = jnp.exp(inter_log - m)
                s = qk * jnp.exp(d_log - m)
                num = _dot(s.astype(BF16), vb) + w_inter * _dot(qb, c_prev)
                qn = jnp.sum(s, axis=1, keepdims=True) + w_inter * jnp.sum(qc * n_prev, axis=1, keepdims=True)
                den = jnp.maximum(jnp.abs(qn), jnp.exp(-m))
                return num * (1.0 / den)

            b_col = cum_c[:, H + h:H + h + 1]
            b_row = cum_r[H + h:H + h + 1, :]
            d_log_f = jnp.where(lower, b_col - b_row + pre_t[h:h + 1, :], NEG)
            m_prev = m_st[0:1, h * HEAD_DIM:h * HEAD_DIM + 1]
            h_f = direction(d_log_f, b_col + m_prev, c_st[h].astype(BF16), n_st[:, hs])

            rb_col = rev_c[:, 3 * H + h:3 * H + h + 1]
            rb_row = rev_r[3 * H + h:3 * H + h + 1, :]
            d_log_b = jnp.where(upper, rb_col - rb_row + pre_t[2 * H + h:2 * H + h + 1, :], NEG)
            h_b = direction(d_log_b, rb_col + m_next[:, h * HEAD_DIM:h * HEAD_DIM + 1],
                            cb_all[c * H + h], n_next[:, hs])

            out_ref[:, hs] = _sigmoid(o_ref[:, hs]) * (h_f + h_b)

            g_tot = b_col[L - 1:L, :]
            state_update(h, kc, vb, g_tot - b_col + pre[:, h:h + 1], g_tot)


def mlstm(proj, gates, gate_bias, conv_w, conv_b, L=ML_L):
    T = proj.shape[0]
    nc = T // L
    hb = L // 8
    n8 = T // 8
    gb = jnp.pad(gate_bias.astype(F32).reshape(1, N_GATE), ((0, 0), (0, LANES - N_GATE)))
    cw = conv_w.astype(F32)
    cb = conv_b.astype(F32).reshape(1, 2 * GROUP_W)

    def chunk(p, jj):
        return p * jj + (1 - p) * (nc - 1 - jj)

    def prev8(p, jj):
        return jnp.maximum(chunk(p, jj) * hb - 1, 0)

    def next8(p, jj):
        return jnp.minimum((chunk(p, jj) + 1) * hb, n8 - 1)

    blk = (L, GROUP_W)
    halo = (8, GROUP_W)
    cq, ck, cv, co = (x // N_HEADS for x in (CB_MQ, CB_MK, CB_MV, CB_MO))
    return pl.pallas_call(
        functools.partial(_ml_kernel, L=L, nc=nc),
        grid=(2, nc),
        in_specs=[
            pl.BlockSpec((L, LANES), lambda p, jj: (chunk(p, jj), 0)),
            pl.BlockSpec((1, LANES), lambda p, jj: (0, 0)),
            pl.BlockSpec(blk, lambda p, jj: (p * jj, cq)),
            pl.BlockSpec(halo, lambda p, jj: (p * prev8(p, jj), cq)),
            pl.BlockSpec(halo, lambda p, jj: (p * next8(p, jj), cq)),
            pl.BlockSpec(blk, lambda p, jj: (chunk(p, jj), ck)),
            pl.BlockSpec(halo, lambda p, jj: (prev8(p, jj), ck)),
            pl.BlockSpec(halo, lambda p, jj: (next8(p, jj), ck)),
            pl.BlockSpec(blk, lambda p, jj: (chunk(p, jj), cv)),
            pl.BlockSpec(blk, lambda p, jj: (p * jj, co)),
            pl.BlockSpec((3, 2 * GROUP_W), lambda p, jj: (0, 0)),
            pl.BlockSpec((1, 2 * GROUP_W), lambda p, jj: (0, 0)),
        ],
        out_specs=pl.BlockSpec(blk, lambda p, jj: (p * jj, 0)),
        out_shape=jax.ShapeDtypeStruct((T, GROUP_W), F32),
        scratch_shapes=[
            pltpu.VMEM((nc * N_HEADS, HEAD_DIM, HEAD_DIM), BF16),
            pltpu.VMEM((nc, GROUP_W), F32),
            pltpu.VMEM((nc, GROUP_W), F32),
            pltpu.VMEM((N_HEADS, HEAD_DIM, HEAD_DIM), F32),
            pltpu.VMEM((1, GROUP_W), F32),
            pltpu.VMEM((1, GROUP_W), F32),
        ],
        compiler_params=_cparams(("arbitrary", "arbitrary"), 48),
        name="mlstm",
    )(gates, gb, proj, proj, proj, proj, proj, proj, proj, proj, cw, cb)


def _s5_unfold_kernel(seq_ref, mt_ref, *, Lc):
    seq = seq_ref[...]
    n = Lc * S5_CH
    for s in range(Lc):
        off = (Lc - s) * S5_CH
        mt_ref[s * S5_CH:(s + 1) * S5_CH, :] = seq[:, off:off + n].astype(BF16)


def s5_unfold(lagseq, Lc):
    G, C, W = lagseq.shape
    K = Lc * C
    return pl.pallas_call(
        functools.partial(_s5_unfold_kernel, Lc=Lc),
        grid=(G,),
        in_specs=[pl.BlockSpec((None, C, W), lambda g: (g, 0, 0))],
        out_specs=pl.BlockSpec((None, K, K), lambda g: (g, 0, 0)),
        out_shape=jax.ShapeDtypeStruct((G, K, K), BF16),
        compiler_params=_cparams(("arbitrary",), 32),
        name="s5_unfold",
    )(lagseq)


def s5_tables(lam_re, lam_im, log_step, b_re, b_im, c_re, c_im, Lc=S5_LC):
    f = lambda t: t.astype(F32)
    lam_re, lam_im, log_step, b_re, b_im, c_re, c_im = map(f, (lam_re, lam_im, log_step, b_re, b_im, c_re, c_im))
    G, P, C = S5_GROUPS, S5_STATE, S5_CH
    step = jnp.exp(log_step)[..., None]
    lre = jnp.minimum(lam_re, LAMBDA_RE_MAX)
    mag = jnp.exp(lre * step)
    ang = lam_im * step
    abar_re, abar_im = mag * jnp.cos(ang), mag * jnp.sin(ang)
    den = lre * lre + lam_im * lam_im
    zr, zi = abar_re - 1.0, abar_im
    coef_re = (zr * lre + zi * lam_im) / den
    coef_im = (zi * lre - zr * lam_im) / den
    bbar_re = coef_re[..., None] * b_re - coef_im[..., None] * b_im
    bbar_im = coef_re[..., None] * b_im + coef_im[..., None] * b_re
    n = jnp.arange(Lc + 1, dtype=F32)
    pmag = jnp.exp((lre * step)[..., None] * n)
    pang = ang[..., None] * n
    pw_re, pw_im = pmag * jnp.cos(pang), pmag * jnp.sin(pang)
    ct_re = jnp.swapaxes(c_re, -1, -2)[..., None]
    ct_im = jnp.swapaxes(c_im, -1, -2)[..., None]
    br_, bi_ = bbar_re[..., None, :], bbar_im[..., None, :]
    cb_re = ct_re * br_ - ct_im * bi_
    cb_im = ct_re * bi_ + ct_im * br_
    hp = lax.Precision.HIGHEST
    kk = (jnp.einsum('xgpn,xgpce->xgnce', pw_re[..., :Lc], cb_re, precision=hp)
          - jnp.einsum('xgpn,xgpce->xgnce', pw_im[..., :Lc], cb_im, precision=hp))
    fwd = jnp.pad(kk[0], ((0, 0), (Lc, 0), (0, 0), (0, 0)))
    bwd = jnp.pad(kk[1][:, ::-1], ((0, 0), (1, Lc - 1), (0, 0), (0, 0)))
    lagseq = (fwd + bwd).transpose(0, 3, 1, 2).reshape(G, C, 2 * Lc * C)
    mt = s5_unfold(lagseq, Lc)
    def summ(pr, pi, br, bi):
        pr, pi = (jnp.swapaxes(t, 1, 2)[:, :, None, :] for t in (pr, pi))
        br, bi = (jnp.swapaxes(t, 1, 2)[:, None, :, :] for t in (br, bi))
        return pr * br - pi * bi, pr * bi + pi * br
    f_re, f_im = summ(pw_re[0][..., Lc - 1::-1][..., :Lc], pw_im[0][..., Lc - 1::-1][..., :Lc], bbar_re[0], bbar_im[0])
    r_re, r_im = summ(pw_re[1][..., :Lc], pw_im[1][..., :Lc], bbar_re[1], bbar_im[1])
    wb = jnp.concatenate([f_re, f_im, r_re, r_im], axis=-1).reshape(G, Lc * C, 4 * P).astype(BF16)
    def read(pr, pi, cr, ci):
        pr, pi = pr[:, :, :, None], pi[:, :, :, None]
        cr, ci = (jnp.swapaxes(t, 1, 2)[:, :, None, :] for t in (cr, ci))
        return cr * pr - ci * pi, -(cr * pi + ci * pr)
    o_re, o_im = read(pw_re[0][..., 1:], pw_im[0][..., 1:], c_re[0], c_im[0])
    q_re, q_im = read(pw_re[1][..., Lc:0:-1], pw_im[1][..., Lc:0:-1], c_re[1], c_im[1])
    wc = jnp.concatenate([o_re, o_im, q_re, q_im], axis=1).reshape(G, 4 * P, Lc * C).astype(BF16)
    al_re, al_im = pw_re[..., Lc], pw_im[..., Lc]
    a1 = jnp.concatenate([al_re, al_re], axis=-1)
    a2 = jnp.concatenate([-al_im, al_im], axis=-1)
    return mt, wb, wc, a1, a2


def _s5_summary_kernel(x_ref, wb_ref, e_ref):
    e_ref[...] = _dot(x_ref[...], wb_ref[...])


def _s5_scan_kernel(ef_ref, eb_ref, a1_ref, a2_ref, sf_ref, sb_ref, *, nc):
    a1f, a2f = a1_ref[0], a2_ref[0]
    a1b, a2b = a1_ref[1], a2_ref[1]
    half = S5_STATE

    def body(j, carry):
        sf, sb = carry
        jb = nc - 1 - j
        sf_ref[j] = sf
        sb_ref[jb] = sb
        sf = a1f * sf + a2f * pltpu.roll(sf, half, 1) + ef_ref[j]
        sb = a1b * sb + a2b * pltpu.roll(sb, half, 1) + eb_ref[jb]
        return sf, sb

    z = jnp.zeros(a1f.shape, F32)
    lax.fori_loop(0, nc, body, (z, z))


def _s5_out_kernel(x_ref, mt_ref, s_ref, wc_ref, y_ref):
    y_ref[...] = _dot(x_ref[...], mt_ref[...]) + _dot(s_ref[...], wc_ref[...])


def s5_core(u_main, tables, Lc=S5_LC):
    mt, wb, wc, a1, a2 = tables
    T = u_main.shape[0]
    nc = T // Lc
    G, C, P = S5_GROUPS, S5_CH, S5_STATE
    K = Lc * C
    x = u_main.astype(BF16).reshape(nc, Lc, G, C).transpose(2, 0, 1, 3).reshape(G, nc, K)
    e = pl.pallas_call(
        _s5_summary_kernel,
        grid=(G,),
        in_specs=[pl.BlockSpec((None, nc, K), lambda g: (g, 0, 0)),
                  pl.BlockSpec((None, K, 4 * P), lambda g: (g, 0, 0))],
        out_specs=pl.BlockSpec((None, nc, 4 * P), lambda g: (g, 0, 0)),
        out_shape=jax.ShapeDtypeStruct((G, nc, 4 * P), F32),
        compiler_params=_cparams(("arbitrary",), 32),
        name="s5_summary",
    )(x, wb)
    et = e.transpose(1, 0, 2)
    ef, eb = et[..., :2 * P], et[..., 2 * P:]
    gs = 16
    sf, sb = pl.pallas_call(
        functools.partial(_s5_scan_kernel, nc=nc),
        grid=(G // gs,),
        in_specs=[pl.BlockSpec((nc, gs, 2 * P), lambda i: (0, i, 0)),
                  pl.BlockSpec((nc, gs, 2 * P), lambda i: (0, i, 0)),
                  pl.BlockSpec((2, gs, 2 * P), lambda i: (0, i, 0)),
                  pl.BlockSpec((2, gs, 2 * P), lambda i: (0, i, 0))],
        out_specs=[pl.BlockSpec((nc, gs, 2 * P), lambda i: (0, i, 0)),
                   pl.BlockSpec((nc, gs, 2 * P), lambda i: (0, i, 0))],
        out_shape=[jax.ShapeDtypeStruct((nc, G, 2 * P), F32)] * 2,
        compiler_params=_cparams(("arbitrary",), 32),
        name="s5_scan",
    )(ef, eb, a1, a2)
    s = jnp.concatenate([sf, sb], axis=-1).transpose(1, 0, 2).astype(BF16)
    y = pl.pallas_call(
        _s5_out_kernel,
        grid=(G,),
        in_specs=[pl.BlockSpec((None, nc, K), lambda g: (g, 0, 0)),
                  pl.BlockSpec((None, K, K), lambda g: (g, 0, 0)),
                  pl.BlockSpec((None, nc, 4 * P), lambda g: (g, 0, 0)),
                  pl.BlockSpec((None, 4 * P, K), lambda g: (g, 0, 0))],
        out_specs=pl.BlockSpec((None, nc, K), lambda g: (g, 0, 0)),
        out_shape=jax.ShapeDtypeStruct((G, nc, K), F32),
        compiler_params=_cparams(("arbitrary",), 32),
        name="s5_out",
    )(x, mt, s, wc)
    return y.reshape(G, nc, Lc, C).transpose(1, 2, 0, 3).reshape(T, GROUP_W)


def _s5_post_kernel(y_ref, u_ref, d_ref, w_ref, b_ref, out_ref):
    y = y_ref[...] + d_ref[...] * u_ref[...]
    y = jax.nn.gelu(y)
    z = _dot(y.astype(BF16), w_ref[...]) + b_ref[...]
    out_ref[...] = y * _sigmoid(z)


def s5_post(y, proj, d_skip, glu_w, glu_b, tm=512):
    T = y.shape[0]
    return pl.pallas_call(
        _s5_post_kernel,
        grid=(T // tm,),
        in_specs=[pl.BlockSpec((tm, GROUP_W), lambda i: (i, 0)),
                  pl.BlockSpec((tm, GROUP_W), lambda i: (i, CB_SU // 8)),
                  pl.BlockSpec((1, GROUP_W), lambda i: (0, 0)),
                  pl.BlockSpec((GROUP_W, GROUP_W), lambda i: (0, 0)),
                  pl.BlockSpec((1, GROUP_W), lambda i: (0, 0))],
        out_specs=pl.BlockSpec((tm, GROUP_W), lambda i: (i, 0)),
        out_shape=jax.ShapeDtypeStruct((T, GROUP_W), F32),
        compiler_params=_cparams(("arbitrary",), 32),
        name="s5_post",
    )(y, proj, d_skip.astype(F32).reshape(1, GROUP_W), glu_w.astype(BF16), glu_b.astype(F32).reshape(1, GROUP_W))


def na_bias_tables(rel_bias):
    rows = 4 * NA_QR
    kh, kw = NA_KH, NA_KW
    H = rel_bias.shape[0]
    n_dr = 2 * NA_KH - 1
    bp = jnp.pad(rel_bias.astype(F32), ((0, 0), (0, 0), (GRID_W - NA_KW, GRID_W - NA_KW)))
    toe = jnp.stack([bp[:, :, GRID_W - 1 - qc:2 * GRID_W - 1 - qc] for qc in range(GRID_W)], axis=2)
    qc = np.arange(GRID_W)[:, None]
    kc = np.arange(GRID_W)[None, :]
    cs = np.clip(qc - kw // 2, 0, GRID_W - kw)
    col_ok = (kc >= cs) & (kc < cs + kw)
    tiles = jnp.where(col_ok[None, None], toe, NEG)
    neg_tile = jnp.full((H, GRID_W, GRID_W), NEG, F32)
    out = []
    for r0 in (0, NA_QR, rows - NA_QR):
        row_blocks = []
        for i in range(NA_QR):
            qr = r0 + i
            rs = min(max(qr - kh // 2, 0), rows - kh)
            parts = []
            for j in range(4 * NA_KR):
                kr = r0 - NA_KR + j
                ok = 0 <= kr < rows and rs <= kr < rs + kh
                parts.append(tiles[:, kr - qr + NA_KH - 1] if ok else neg_tile)
            row_blocks.append(jnp.concatenate(parts, axis=2))
        out.append(jnp.concatenate(row_blocks, axis=1))
    return jnp.stack(out, axis=1)


def _na_kernel(q_ref, k0, k1, k2, k3, v0, v1, v2, v3, tab_ref, out_ref, *, kb):
    q = (q_ref[...] * (HEAD_DIM ** -0.5)).astype(BF16)
    ks = (k0, k1, k2, k3)
    vs = (v0, v1, v2, v3)
    s = jnp.concatenate([_dot_nt(q, kr[...].astype(BF16)) for kr in ks], axis=1) + tab_ref[...]
    m = jnp.max(s, axis=1, keepdims=True)
    p = jnp.exp(s - m)
    l = jnp.sum(p, axis=1, keepdims=True)
    pb = p.astype(BF16)
    o = _dot(pb[:, 0:kb], vs[0][...].astype(BF16))
    for i in range(1, 4):
        o += _dot(pb[:, i * kb:(i + 1) * kb], vs[i][...].astype(BF16))
    out_ref[...] = o * (1.0 / l)


def neighbourhood_attention(proj, tables):
    T = proj.shape[0]
    rows = T // GRID_W
    assert rows >= 3 * NA_QR and rows % NA_QR == 0
    tq = NA_QR * GRID_W
    kb = NA_KR * GRID_W
    nb = T // tq
    nkb = T // kb

    def kidx(b, i):
        return jnp.clip(2 * b - 1 + i, 0, nkb - 1)

    def pat(b):
        return jnp.where(b == 0, 0, jnp.where(b == nb - 1, 2, 1))

    kspecs = [pl.BlockSpec((kb, HEAD_DIM), functools.partial(lambda h, b, i: (kidx(b, i), CB_NK + h), i=i))
              for i in range(4)]
    vspecs = [pl.BlockSpec((kb, HEAD_DIM), functools.partial(lambda h, b, i: (kidx(b, i), CB_NV + h), i=i))
              for i in range(4)]
    return pl.pallas_call(
        functools.partial(_na_kernel, kb=kb),
        grid=(N_HEADS, nb),
        in_specs=[pl.BlockSpec((tq, HEAD_DIM), lambda h, b: (b, CB_NQ + h))] + kspecs + vspecs + [
            pl.BlockSpec((None, None, tq, 4 * kb), lambda h, b: (h, pat(b), 0, 0))],
        out_specs=pl.BlockSpec((tq, HEAD_DIM), lambda h, b: (b, h)),
        out_shape=jax.ShapeDtypeStruct((T, GROUP_W), F32),
        compiler_params=_cparams(("arbitrary", "arbitrary"), 40),
        name="neighbourhood_attention",
    )(proj, *([proj] * 8), tables)


def _out_proj_kernel(a0, a1, a2, a3, g_ref, w_ref, h_ref, out_ref, m_ref):
    j = pl.program_id(1)

    @pl.when(j == 0)
    def _():
        for gi, a in enumerate((a0, a1, a2, a3)):
            for r in range(0, a.shape[0], ROW_CHUNK):
                rows = slice(r, r + ROW_CHUNK)
                m_ref[rows, gi * GROUP_W:(gi + 1) * GROUP_W] = _rms(a[rows, :], g_ref[gi:gi + 1, :]).astype(BF16)

    out_ref[...] = h_ref[...] + _dot(m_ref[...], w_ref[...])


def out_proj(groups, gain, w_out, h, tm=512, tn=1024):
    T, D = h.shape
    gspec = pl.BlockSpec((tm, GROUP_W), lambda i, j: (i, 0))
    return pl.pallas_call(
        _out_proj_kernel,
        grid=(T // tm, D // tn),
        in_specs=[gspec, gspec, gspec, gspec,
                  pl.BlockSpec((4, GROUP_W), lambda i, j: (0, 0)),
                  pl.BlockSpec((D, tn), lambda i, j: (0, j)),
                  pl.BlockSpec((tm, tn), lambda i, j: (i, j))],
        out_specs=pl.BlockSpec((tm, tn), lambda i, j: (i, j)),
        out_shape=jax.ShapeDtypeStruct((T, D), F32),
        scratch_shapes=[pltpu.VMEM((tm, D), BF16)],
        compiler_params=_cparams(("arbitrary", "arbitrary"), 52),
        name="out_proj",
    )(*groups, gain, w_out, h)


def _router_kernel(h_ref, g_ref, wrh_ref, wrl_ref, xn_ref, aff_ref):
    for r in range(0, h_ref.shape[0], ROW_CHUNK):
        rows = slice(r, r + ROW_CHUNK)
        xn = _rms(h_ref[rows, :], g_ref[...])
        xn_ref[rows, :] = xn
        xh = xn.astype(BF16)
        xl = (xn - xh.astype(F32)).astype(BF16)
        logits = _dot(xh, wrh_ref[...]) + _dot(xl, wrh_ref[...]) + _dot(xh, wrl_ref[...])
        lane = lax.broadcasted_iota(jnp.int32, logits.shape, 1)
        logits = jnp.where(lane < N_EXPERTS, logits, NEG)
        m = jnp.max(logits, axis=1, keepdims=True)
        e = jnp.exp(logits - m)
        aff_ref[rows, :] = e / jnp.sum(e, axis=1, keepdims=True)


def router(h, gain, w_router_pad, tm=512):
    T, D = h.shape
    wr_hi = w_router_pad.astype(BF16)
    wr_lo = (w_router_pad - wr_hi.astype(F32)).astype(BF16)
    return pl.pallas_call(
        _router_kernel,
        grid=(T // tm,),
        in_specs=[pl.BlockSpec((tm, D), lambda i: (i, 0)),
                  pl.BlockSpec((1, D), lambda i: (0, 0)),
                  pl.BlockSpec((D, LANES), lambda i: (0, 0)),
                  pl.BlockSpec((D, LANES), lambda i: (0, 0))],
        out_specs=[pl.BlockSpec((tm, D), lambda i: (i, 0)),
                   pl.BlockSpec((tm, LANES), lambda i: (i, 0))],
        out_shape=[jax.ShapeDtypeStruct((T, D), F32), jax.ShapeDtypeStruct((T, LANES), F32)],
        compiler_params=_cparams(("arbitrary",), 48),
        name="router",
    )(h, gain, wr_hi, wr_lo)


def _moe_up_kernel(idx_ref, x_hbm, wg_ref, wu_ref, hid_ref, xbuf, xb16, sem, *, tm, n_tiles):
    e = pl.program_id(0)
    m = pl.program_id(1)
    f = pl.program_id(2)
    n = e * pl.num_programs(1) + m
    slot = n % 2

    def row_copy(tok, r, s):
        return pltpu.make_async_copy(x_hbm.at[pl.ds(tok, 1), :], xbuf.at[s, pl.ds(r, 1), :], sem.at[s])

    def issue(tile, s):
        base = tile * tm

        def body(r, c):
            row_copy(idx_ref[base + r], r, s).start()
            return c

        lax.fori_loop(0, tm, body, 0, unroll=8)

    def wait(s):
        def body(r, c):
            row_copy(0, r, s).wait()
            return c

        lax.fori_loop(0, tm, body, 0, unroll=8)

    @pl.when(f == 0)
    def _():
        @pl.when(n == 0)
        def _():
            issue(0, 0)

        @pl.when(n + 1 < n_tiles)
        def _():
            issue(n + 1, 1 - slot)

        wait(slot)
        xb16[...] = xbuf[slot].astype(BF16)

    x = xb16[...]
    a = _dot(x, wg_ref[...])
    b = _dot(x, wu_ref[...])
    hid_ref[...] = (a * _sigmoid(a) * b).astype(BF16)


def moe_up(idx_flat, xn, wg, wu, cap, tm=MOE_TM, tf=512):
    E, D, F = wg.shape
    n_m = cap // tm
    grid_spec = pltpu.PrefetchScalarGridSpec(
        num_scalar_prefetch=1,
        grid=(E, n_m, F // tf),
        in_specs=[pl.BlockSpec(memory_space=pl.ANY),
                  pl.BlockSpec((None, D, tf), lambda e, m, f, idx: (e, 0, f)),
                  pl.BlockSpec((None, D, tf), lambda e, m, f, idx: (e, 0, f))],
        out_specs=pl.BlockSpec((None, tm, tf), lambda e, m, f, idx: (e, m, f)),
        scratch_shapes=[pltpu.VMEM((2, tm, D), F32), pltpu.VMEM((tm, D), BF16), pltpu.SemaphoreType.DMA((2,))],
    )
    return pl.pallas_call(
        functools.partial(_moe_up_kernel, tm=tm, n_tiles=E * n_m),
        grid_spec=grid_spec,
        out_shape=jax.ShapeDtypeStruct((E, cap, F), BF16),
        compiler_params=_cparams(("arbitrary", "arbitrary", "arbitrary"), 48),
        name="moe_up",
    )(idx_flat, xn, wg, wu)


def _moe_down_kernel(hid_ref, wd_ref, gate_ref, ye_ref, *, rc):
    wb = wd_ref[...].astype(BF16)
    for r in range(0, hid_ref.shape[0], rc):
        rows = slice(r, r + rc)
        ye_ref[rows, :] = (_dot(hid_ref[rows, :], wb) * gate_ref[rows, :]).astype(BF16)


def moe_down(hid, wd, gate, tn=512, rc=512):
    E, cap, F = hid.shape
    D = wd.shape[2]
    rc = min(rc, cap)
    return pl.pallas_call(
        functools.partial(_moe_down_kernel, rc=rc),
        grid=(E, D // tn),
        in_specs=[pl.BlockSpec((None, cap, F), lambda e, n: (e, 0, 0)),
                  pl.BlockSpec((None, F, tn), lambda e, n: (e, 0, n)),
                  pl.BlockSpec((None, cap, 1), lambda e, n: (e, 0, 0))],
        out_specs=pl.BlockSpec((None, cap, tn), lambda e, n: (e, 0, n)),
        out_shape=jax.ShapeDtypeStruct((E, cap, D), BF16),
        compiler_params=_cparams(("arbitrary", "arbitrary"), 48),
        name="moe_down",
    )(hid, wd, gate)


def _combine_kernel(st_ref, h_ref, *refs, tb, W, nbp1, n_win, cn, eg):
    idx_refs = refs[:2 * eg]
    y_refs = refs[2 * eg:4 * eg]
    idx_hbm, ye_hbm, out_ref, ybuf, ibuf, sem = refs[4 * eg:]
    b = pl.program_id(0)
    half = pl.program_id(1)
    ep = pl.program_id(2)
    D = out_ref.shape[1]

    @pl.when(ep == 0)
    def _():
        out_ref[...] = h_ref[...]

    tok = b * tb + lax.broadcasted_iota(jnp.int32, (tb, 1), 0)

    def onehot(idx_row, valid):
        return jnp.where(tok == idx_row * valid + (valid - 1), 1.0, 0.0).astype(BF16)

    ohs = []
    for k in range(eg):
        wa = st_ref[(ep * eg + k) * nbp1 + b] // W
        ohs.append(onehot(idx_refs[2 * k][...], (wa < n_win).astype(jnp.int32)))
        ohs.append(onehot(idx_refs[2 * k + 1][...], (wa + 1 < n_win).astype(jnp.int32)))
    oh = jnp.concatenate(ohs, axis=1)
    for c0 in range(0, D, cn):
        ycat = jnp.concatenate([y[:, c0:c0 + cn] for y in y_refs], axis=0)
        out_ref[:, c0:c0 + cn] += _dot(oh, ycat)

    wo = ybuf.shape[0]
    for k in range(eg):
        e = ep * eg + k
        lim = (st_ref[e * nbp1 + b] // W + 2) * W
        s1 = st_ref[e * nbp1 + b + 1]

        @pl.when(s1 > lim)
        def _(e=e, lim=lim, s1=s1):
            first = lim // wo

            def body(w, c):
                off = pl.multiple_of((first + w) * wo, wo)
                col = pl.multiple_of(half * D, D)
                cy = pltpu.make_async_copy(ye_hbm.at[e, pl.ds(off, wo), pl.ds(col, D)], ybuf, sem.at[0])
                ci = pltpu.make_async_copy(idx_hbm.at[e, :, pl.ds(off, wo)], ibuf, sem.at[1])
                cy.start()
                ci.start()
                cy.wait()
                ci.wait()
                slot = off + lax.broadcasted_iota(jnp.int32, (1, wo), 1)
                ohw = onehot(jnp.where(slot >= lim, ibuf[...], -1), 1)
                for c0 in range(0, D, cn):
                    out_ref[:, c0:c0 + cn] += _dot(ohw, ybuf[:, c0:c0 + cn])
                return c

            lax.fori_loop(0, (s1 + wo - 1) // wo - first, body, 0)


def moe_combine(starts_flat, h, idx_s, ye, tb=COMB_TB, W=COMB_W):
    T, D = h.shape
    E, cap, _ = ye.shape
    nb = T // tb
    n_win = cap // W
    idx4 = idx_s.reshape(E, n_win, 1, W)
    idx3 = idx_s.reshape(E, 1, cap)

    eg = COMB_EG

    dc = D // COMB_NH

    def win(b, ep, st, k, j):
        return jnp.minimum(st[(ep * eg + k) * (nb + 1) + b] // W + j, n_win - 1)

    idx_specs, y_specs = [], []
    for k in range(eg):
        for j in range(2):
            idx_specs.append(pl.BlockSpec(
                (None, None, 1, W),
                functools.partial(lambda b, c, ep, st, k, j: (ep * eg + k, win(b, ep, st, k, j), 0, 0), k=k, j=j)))
            y_specs.append(pl.BlockSpec(
                (None, W, dc),
                functools.partial(lambda b, c, ep, st, k, j: (ep * eg + k, win(b, ep, st, k, j), c), k=k, j=j)))
    grid_spec = pltpu.PrefetchScalarGridSpec(
        num_scalar_prefetch=1,
        grid=(nb, COMB_NH, E // eg),
        in_specs=[pl.BlockSpec((tb, dc), lambda b, c, ep, st: (b, c), pipeline_mode=pl.Buffered(1))]
        + idx_specs + y_specs + [pl.BlockSpec(memory_space=pl.ANY), pl.BlockSpec(memory_space=pl.ANY)],
        out_specs=pl.BlockSpec((tb, dc), lambda b, c, ep, st: (b, c)),
        scratch_shapes=[pltpu.VMEM((LANES, dc), BF16), pltpu.VMEM((1, LANES), jnp.int32),
                        pltpu.SemaphoreType.DMA((2,))],
    )
    return pl.pallas_call(
        functools.partial(_combine_kernel, tb=tb, W=W, nbp1=nb + 1, n_win=n_win, cn=1024, eg=eg),
        grid_spec=grid_spec,
        out_shape=jax.ShapeDtypeStruct((T, D), F32),
        compiler_params=_cparams(("arbitrary", "arbitrary", "arbitrary"), 48),
        name="moe_combine",
    )(starts_flat, h, *([idx4] * (2 * eg)), *([ye] * (2 * eg)), idx3, ye)


def _select_kernel(a_ref, thr_ref, ngt_ref, *, cap):
    bits = pltpu.bitcast(a_ref[...], jnp.int32)
    thr = jnp.zeros((bits.shape[0], 1), jnp.int32)
    for b in range(30, -1, -1):
        cand = thr | (1 << b)
        cnt = jnp.sum(jnp.where(bits >= cand, 1.0, 0.0), axis=1, keepdims=True)
        thr = jnp.where(cnt >= cap, cand, thr)
    ngt = jnp.sum(jnp.where(bits > thr, 1.0, 0.0), axis=1, keepdims=True)
    thr_ref[...] = jnp.broadcast_to(thr, thr_ref.shape)
    ngt_ref[...] = jnp.broadcast_to(ngt.astype(jnp.int32), ngt_ref.shape)


def expert_select(aff_t, cap):
    E, T = aff_t.shape
    thr, ngt = pl.pallas_call(
        functools.partial(_select_kernel, cap=cap),
        grid=(1,),
        in_specs=[pl.BlockSpec((E, T), lambda i: (0, 0))],
        out_specs=[pl.BlockSpec((E, LANES), lambda i: (0, 0)), pl.BlockSpec((E, LANES), lambda i: (0, 0))],
        out_shape=[jax.ShapeDtypeStruct((E, LANES), jnp.int32)] * 2,
        compiler_params=_cparams(("arbitrary",), 32),
        name="expert_select",
    )(aff_t)
    bits = lax.bitcast_convert_type(aff_t, jnp.int32)
    thr, ngt = thr[:, :1], ngt[:, :1]
    tie = bits == thr
    take = (bits > thr) | (tie & (jnp.cumsum(tie.astype(jnp.int32), axis=1) <= cap - ngt))
    tok = jnp.arange(T, dtype=jnp.int32)[None, :]
    idx = jnp.sort(jnp.where(take, tok, tok + T), axis=1)[:, :cap]
    return idx, jnp.take_along_axis(aff_t, idx, axis=1)


def expert_choice_ffn(h, gain, w_router_pad, wg, wu, wd):
    T, D = h.shape
    E = N_EXPERTS
    cap = EC_CAPACITY * T // E
    xn, aff = router(h, gain, w_router_pad)
    idx_s, gate_s = expert_select(aff[:, :E].T, cap)
    hid = moe_up(idx_s.reshape(-1), xn, wg, wu, cap)
    ye = moe_down(hid, wd, gate_s[..., None])
    nb = T // COMB_TB
    edges = (jnp.arange(nb + 1, dtype=jnp.int32) * COMB_TB)
    starts = jnp.sum(idx_s[:, :, None] < edges[None, None, :], axis=1).astype(jnp.int32)
    return moe_combine(starts.reshape(-1), h, idx_s, ye)


def _ple_kernel(h_ref, g_ref, p_ref, wp_ref, wg_ref, gf_ref, out_ref, hn_ref, *, tn, final):
    j = pl.program_id(1)

    tm = h_ref.shape[0]

    @pl.when(j == 0)
    def _():
        for r in range(0, tm, ROW_CHUNK):
            rows = slice(r, r + ROW_CHUNK)
            hn_ref[rows, :] = _rms(h_ref[rows, :], g_ref[...]).astype(BF16)

    gate = _sigmoid(_dot(hn_ref[...], wg_ref[...]))
    emb = _dot(p_ref[...].astype(BF16), wp_ref[...])
    c0 = pl.multiple_of(j * tn, tn)
    out_ref[:, pl.ds(c0, tn)] = h_ref[:, pl.ds(c0, tn)] + emb * gate

    if final:
        @pl.when(j == pl.num_programs(1) - 1)
        def _():
            for r in range(0, tm, ROW_CHUNK):
                rows = slice(r, r + ROW_CHUNK)
                out_ref[rows, :] = _rms(out_ref[rows, :], gf_ref[...])


def ple(h, gain, p, w_ple, w_gate, gain_final, final, tm=512, tn=512):
    T, D = h.shape
    return pl.pallas_call(
        functools.partial(_ple_kernel, tn=tn, final=final),
        grid=(T // tm, D // tn),
        in_specs=[pl.BlockSpec((tm, D), lambda i, j: (i, 0)),
                  pl.BlockSpec((1, D), lambda i, j: (0, 0)),
                  pl.BlockSpec((tm, PLE_DIM), lambda i, j: (i, 0)),
                  pl.BlockSpec((PLE_DIM, tn), lambda i, j: (0, j)),
                  pl.BlockSpec((D, tn), lambda i, j: (0, j)),
                  pl.BlockSpec((1, D), lambda i, j: (0, 0))],
        out_specs=pl.BlockSpec((tm, D), lambda i, j: (i, 0)),
        out_shape=jax.ShapeDtypeStruct((T, D), F32),
        scratch_shapes=[pltpu.VMEM((tm, D), BF16)],
        compiler_params=_cparams(("arbitrary", "arbitrary"), 52),
        name="ple",
    )(h, gain, p, w_ple, w_gate, gain_final)


def rotary_tables(T):
    half = HEAD_DIM // 2
    inv_freq = ROPE_BASE ** (-jnp.arange(half, dtype=F32) / half)
    ang = jnp.arange(T, dtype=F32)[:, None] * inv_freq[None, :]
    cos, sin = jnp.cos(ang), jnp.sin(ang)
    return jnp.concatenate([cos, cos], axis=1), jnp.concatenate([-sin, sin], axis=1)


def kernel(x_prompt, x_sample, p_prompt, p_sample, norm_mix, w_in, ret_decay_logit, mlstm_conv_w, mlstm_conv_b, mlstm_gate_bias, s5_lambda_re, s5_lambda_im, s5_log_step, s5_b_re, s5_b_im, s5_c_re, s5_c_im, s5_d, s5_glu_w, s5_glu_b, na_rel_bias, mix_out_norm, w_out, norm_ffn, w_router, w_expert_gate, w_expert_up, w_expert_down, ple_norm, w_ple, w_ple_gate, norm_final):
    depth = w_in.shape[0]
    gate_lo = 8 * GROUP_W
    layers = []
    for l in range(depth):
        wl = w_in[l]
        w_main = jnp.concatenate([wl[:, :gate_lo], wl[:, gate_lo + N_GATE:]], axis=1).astype(BF16)
        wgp = jnp.pad(wl[:, gate_lo:gate_lo + N_GATE].astype(F32), ((0, 0), (0, LANES - N_GATE)))
        wg_hi = wgp.astype(BF16)
        wg_lo = (wgp - wg_hi.astype(F32)).astype(BF16)
        layers.append(dict(
            norm_mix=norm_mix[l].astype(F32).reshape(1, D_MODEL),
            w_main=w_main, wg_hi=wg_hi, wg_lo=wg_lo,
            s5=s5_tables(s5_lambda_re[l], s5_lambda_im[l], s5_log_step[l], s5_b_re[l], s5_b_im[l],
                         s5_c_re[l], s5_c_im[l]),
            na=na_bias_tables(na_rel_bias[l]),
            mix_gain=mix_out_norm[l].astype(F32).reshape(4, GROUP_W),
            w_out=w_out[l].astype(BF16),
            norm_ffn=norm_ffn[l].astype(F32).reshape(1, D_MODEL),
            w_router=jnp.pad(w_router[l].astype(F32), ((0, 0), (0, LANES - N_EXPERTS))),
            wg=w_expert_gate[l].astype(BF16), wu=w_expert_up[l].astype(BF16), wd=w_expert_down[l].astype(F32),
            ple_norm=ple_norm[l].astype(F32).reshape(1, D_MODEL),
            w_ple=w_ple[l].astype(BF16), w_ple_gate=w_ple_gate[l].astype(BF16),
        ))
    g_final = norm_final.astype(F32).reshape(1, D_MODEL)

    def run(x, p):
        B, T, _ = x.shape
        assert B == 1
        h = x.reshape(T, D_MODEL)
        cos_t, sin_t = rotary_tables(T)
        for l, lw in enumerate(layers):
            proj, gates = in_proj(h, lw["norm_mix"], lw["w_main"], lw["wg_hi"], lw["wg_lo"])
            o_ret = retention(proj, ret_decay_logit[l], cos_t, sin_t)
            o_ml = mlstm(proj, gates, mlstm_gate_bias[l], mlstm_conv_w[l], mlstm_conv_b[l])
            y_s5 = s5_core(proj[:, CB_SU * LANES:CB_SU * LANES + GROUP_W], lw["s5"])
            o_s5 = s5_post(y_s5, proj, s5_d[l], s5_glu_w[l], s5_glu_b[l])
            o_na = neighbourhood_attention(proj, lw["na"])
            h = out_proj((o_ret, o_ml, o_s5, o_na), lw["mix_gain"], lw["w_out"], h)
            h = expert_choice_ffn(h, lw["norm_ffn"], lw["w_router"], lw["wg"], lw["wu"], lw["wd"])
            h = ple(h, lw["ple_norm"], p[l].reshape(T, PLE_DIM), lw["w_ple"], lw["w_ple_gate"], g_final,
                    final=(l == depth - 1))
        return h.reshape(B, T, D_MODEL)

    return (run(x_prompt, p_prompt), run(x_sample, p_sample))
```
